```python
import jax, jax.numpy as jnp
from jax import lax
import numpy as np

D_MODEL = 2048
BATCH = 4
SEQ = 4096
DEPTH = 1

D_MIX = D_MODEL
D_GMLP = D_MIX // 2
D_ATTN = D_MIX - D_GMLP
CHUNK = 128
GMLP_GROUPS = 8
GMLP_GROUP_DIM = D_GMLP // GMLP_GROUPS
HEAD_DIM = 64
N_Q_HEADS = D_ATTN // HEAD_DIM
N_KV_HEADS = 2
GQA_GROUP = N_Q_HEADS // N_KV_HEADS
D_KV = N_KV_HEADS * HEAD_DIM
WINDOW = 128
BLOCK = 128
ROPE_THETA = 10000.0
EPS = 1e-6

_SIZES = [D_GMLP, D_GMLP, D_GMLP, D_ATTN, D_KV, D_KV, D_ATTN]
_OFFS = list(np.cumsum(_SIZES)[:-1].tolist())
D_IN_PROJ = int(sum(_SIZES))
D_QKV = D_ATTN + 2 * D_KV
QKV_START = 3 * D_GMLP

kernel_name = "hybrid_gmlp_swa_sink_layer"


def rms_norm(x, g):
    xf = x.astype(jnp.float32)
    y = xf * lax.rsqrt(jnp.mean(xf * xf, axis=-1, keepdims=True) + EPS)
    return (y * g.astype(jnp.float32)).astype(x.dtype)


def layer_norm(x, g, b):
    xf = x.astype(jnp.float32)
    mu = jnp.mean(xf, axis=-1, keepdims=True)
    var = jnp.mean(jnp.square(xf - mu), axis=-1, keepdims=True)
    y = (xf - mu) * lax.rsqrt(var + EPS) * g.astype(jnp.float32) + b.astype(jnp.float32)
    return y.astype(x.dtype)


def rope(x, positions):
    half = HEAD_DIM // 2
    inv_freq = ROPE_THETA ** (-jnp.arange(half, dtype=jnp.float32) * (2.0 / HEAD_DIM))
    ang = positions.astype(jnp.float32)[..., None] * inv_freq
    cos = jnp.cos(ang)[:, :, None, :]
    sin = jnp.sin(ang)[:, :, None, :]
    xf = x.astype(jnp.float32)
    x1, x2 = xf[..., :half], xf[..., half:]
    return jnp.concatenate([x1 * cos - x2 * sin, x2 * cos + x1 * sin], axis=-1).astype(x.dtype)


def chunked_spatial_gating(u, v, ln_g, ln_b, w_s, b_s):
    B, S, _ = u.shape
    nc = S // CHUNK
    vn = layer_norm(v, ln_g, ln_b).reshape(B, nc, CHUNK, GMLP_GROUPS, GMLP_GROUP_DIM)
    causal = jnp.tril(jnp.ones((CHUNK, CHUNK), dtype=bool))
    w = w_s * causal.astype(w_s.dtype)
    mixed = jnp.einsum('gts,bnsgc->bntgc', w, vn)
    mixed = mixed + jnp.transpose(b_s)[None, None, :, :, None]
    return u * mixed.reshape(B, S, D_GMLP)


def sliding_window_attention(q, k, v, sinks):
    B, S, _, _ = q.shape
    nb = S // BLOCK
    qb = q.reshape(B, nb, BLOCK, N_KV_HEADS, GQA_GROUP, HEAD_DIM)
    kb = k.reshape(B, nb, BLOCK, N_KV_HEADS, HEAD_DIM)
    vb = v.reshape(B, nb, BLOCK, N_KV_HEADS, HEAD_DIM)
    pad = ((0, 0), (1, 0), (0, 0), (0, 0), (0, 0))
    kk = jnp.concatenate([jnp.pad(kb, pad)[:, :-1], kb], axis=2)
    vv = jnp.concatenate([jnp.pad(vb, pad)[:, :-1], vb], axis=2)
    scores = jnp.einsum('bnqhgd,bnkhd->bnhgqk', qb, kk,
                        preferred_element_type=jnp.float32) * (HEAD_DIM ** -0.5)
    qi = jnp.arange(BLOCK)[:, None] + BLOCK
    kj = jnp.arange(2 * BLOCK)[None, :]
    dist = qi - kj
    band = (dist >= 0) & (dist < WINDOW)
    blk = jnp.arange(nb)[:, None, None]
    valid = band[None] & ((blk * BLOCK + kj[None] - BLOCK) >= 0)
    scores = jnp.where(valid[None, :, None, None], scores, -jnp.inf)
    sink = sinks.astype(jnp.float32).reshape(N_KV_HEADS, GQA_GROUP)[None, None, :, :, None, None]
    m = jnp.maximum(jnp.max(scores, axis=-1, keepdims=True), sink)
    p = jnp.exp(scores - m)
    denom = jnp.sum(p, axis=-1, keepdims=True) + jnp.exp(sink - m)
    p = (p / denom).astype(v.dtype)
    out = jnp.einsum('bnhgqk,bnkhd->bnqhgd', p, vv)
    return out.reshape(B, S, N_Q_HEADS * HEAD_DIM)


def setup_inputs(seed: int = 0) -> dict:
    key = jax.random.key(seed)
    ks = jax.random.split(key, 12)
    f32 = jnp.float32
    x = jax.random.normal(ks[0], (BATCH, SEQ, D_MODEL), f32)
    offsets = jax.random.randint(ks[1], (BATCH, 1), 0, 1024, dtype=jnp.int32)
    positions = (jnp.arange(SEQ, dtype=jnp.int32)[None, :] + offsets).astype(jnp.int32)
    g_pre = 1.0 + 0.02 * jax.random.normal(ks[2], (DEPTH, D_MODEL), f32)
    w_in = jax.random.normal(ks[3], (DEPTH, D_MODEL, D_IN_PROJ), f32) * (D_MODEL ** -0.5)
    b_qkv = 0.01 * jax.random.normal(ks[4], (DEPTH, D_QKV), f32)
    ln_v_g = 1.0 + 0.02 * jax.random.normal(ks[5], (DEPTH, D_GMLP), f32)
    ln_v_b = 0.01 * jax.random.normal(ks[6], (DEPTH, D_GMLP), f32)
    w_spatial = jax.random.normal(ks[7], (DEPTH, GMLP_GROUPS, CHUNK, CHUNK), f32) * (CHUNK ** -0.5)
    b_spatial = 1.0 + 0.02 * jax.random.normal(ks[8], (DEPTH, GMLP_GROUPS, CHUNK), f32)
    attn_sinks = jax.random.normal(ks[9], (DEPTH, N_Q_HEADS), f32)
    w_out = jax.random.normal(ks[10], (DEPTH, D_MIX, D_MODEL), f32) * (D_MIX ** -0.5)
    g_post = 1.0 + 0.02 * jax.random.normal(ks[11], (DEPTH, D_MODEL), f32)
    return {"x": x, "positions": positions, "g_pre": g_pre, "w_in": w_in, "b_qkv": b_qkv,
            "ln_v_g": ln_v_g, "ln_v_b": ln_v_b, "w_spatial": w_spatial, "b_spatial": b_spatial,
            "attn_sinks": attn_sinks, "w_out": w_out, "g_post": g_post}


def reference(x, positions, g_pre, w_in, b_qkv, ln_v_g, ln_v_b, w_spatial, b_spatial,
              attn_sinks, w_out, g_post):
    B, S, _ = x.shape
    for l in range(DEPTH):
        h = rms_norm(x, g_pre[l])
        proj = jnp.einsum('bsd,de->bse', h, w_in[l])
        bias = jnp.concatenate([jnp.zeros((QKV_START,), proj.dtype), b_qkv[l].astype(proj.dtype),
                                jnp.zeros((D_ATTN,), proj.dtype)])
        proj = proj + bias
        u, v_g, z_a, q, k, v_a, z_b = jnp.split(proj, _OFFS, axis=-1)
        y_a = chunked_spatial_gating(u, v_g, ln_v_g[l], ln_v_b[l], w_spatial[l], b_spatial[l])
        y_a = y_a * jax.nn.silu(z_a)
        q = rope(q.reshape(B, S, N_Q_HEADS, HEAD_DIM), positions)
        k = rope(k.reshape(B, S, N_KV_HEADS, HEAD_DIM), positions)
        v_a = v_a.reshape(B, S, N_KV_HEADS, HEAD_DIM)
        y_b = sliding_window_attention(q, k, v_a, attn_sinks[l]) * jax.nn.silu(z_b)
        y = jnp.einsum('bse,ed->bsd', jnp.concatenate([y_a, y_b], axis=-1), w_out[l])
        x = x + rms_norm(y, g_post[l])
    return x
```

```python
import functools

import numpy as np
import jax
import jax.numpy as jnp
from jax import lax
from jax.experimental import pallas as pl
from jax.experimental.pallas import tpu as pltpu

F32 = jnp.float32
BF16 = jnp.bfloat16

D_MODEL = 2048
D_GMLP = 1024
D_ATTN = 1024
CHUNK = 128
GROUPS = 8
HEAD_DIM = 64
HALF = HEAD_DIM // 2
N_Q_HEADS = 16
N_KV_HEADS = 2
N_PAIRS = N_Q_HEADS // 2
PAIRS_PER_KV = N_PAIRS // N_KV_HEADS
D_KV = N_KV_HEADS * HEAD_DIM
D_IN_PROJ = 3 * D_GMLP + D_ATTN + 2 * D_KV + D_ATTN
ROPE_THETA = 10000.0
EPS = 1e-6
LANES = 128

OFF_U, OFF_VG, OFF_ZA, OFF_Q, OFF_ZB = 0, 1024, 2048, 3072, 4096
OFF_KV = 5120

VMEM_LIMIT = 56 * 1024 * 1024


def _pair_layout(t, n_pairs):
    lead = t.shape[:-1]
    t = t.reshape(lead + (n_pairs, 2, 2, HALF))
    return jnp.swapaxes(t, -3, -2).reshape(lead + (n_pairs * 2 * HEAD_DIM,))


def _permute_proj_columns(t):
    q0 = 3 * D_GMLP
    k0 = q0 + D_ATTN
    v0 = k0 + D_KV
    z0 = v0 + D_KV
    return jnp.concatenate([
        t[..., :q0],
        _pair_layout(t[..., q0:k0], N_PAIRS),
        t[..., z0:],
        _pair_layout(t[..., k0:v0], 1),
        t[..., v0:z0],
    ], axis=-1)


def _rope_tables_kernel(pos_ref, invf_ref, cos_ref, sin_ref):
    ang = pos_ref[...].astype(F32) * invf_ref[...]
    cos_ref[...] = jnp.cos(ang)
    sin_ref[...] = jnp.sin(ang)


def _rope_tables(positions):
    n_tok = positions.size
    inv_freq = ROPE_THETA ** (-jnp.arange(HALF, dtype=F32) * (2.0 / HEAD_DIM))
    per_row = LANES // HALF
    pos_rep = jnp.repeat(positions.reshape(n_tok // per_row, per_row), HALF, axis=1)
    invf = jnp.tile(inv_freq, per_row).reshape(1, LANES)
    rows = n_tok // per_row
    cos_c, sin_c = pl.pallas_call(
        _rope_tables_kernel,
        out_shape=(jax.ShapeDtypeStruct((rows, LANES), F32),) * 2,
        name="rope_tables",
    )(pos_rep, invf)
    expand = lambda t: jnp.tile(t.reshape(n_tok, HALF), (1, per_row))
    return expand(cos_c), expand(sin_c)


IN_TM = 512
IN_TN = 768
IN_ROWS = 128


def _in_proj_kernel(x_ref, g_ref, w_ref, b_ref, o_ref, h_ref):
    def norm_rows(r, carry):
        rows = pl.ds(pl.multiple_of(r * IN_ROWS, IN_ROWS), IN_ROWS)
        x = x_ref[rows, :]
        y = x * lax.rsqrt(jnp.mean(x * x, axis=-1, keepdims=True) + EPS)
        h_ref[rows, :] = (y * g_ref[...]).astype(BF16)
        return carry

    lax.fori_loop(0, IN_TM // IN_ROWS, norm_rows, 0)
    for c in range(D_IN_PROJ // IN_TN):
        cols = slice(c * IN_TN, (c + 1) * IN_TN)
        acc = jnp.dot(h_ref[...], w_ref[:, cols], preferred_element_type=F32)
        o_ref[:, cols] = (acc + b_ref[:, cols]).astype(BF16)


def _in_proj(x2, g_pre, w_in_p, bias_p):
    n_tok = x2.shape[0]
    const = dict(pipeline_mode=pl.Buffered(1))
    return pl.pallas_call(
        _in_proj_kernel,
        grid=(n_tok // IN_TM,),
        in_specs=[
            pl.BlockSpec((IN_TM, D_MODEL), lambda i: (i, 0)),
            pl.BlockSpec((1, D_MODEL), lambda i: (0, 0), **const),
            pl.BlockSpec((D_MODEL, D_IN_PROJ), lambda i: (0, 0), **const),
            pl.BlockSpec((1, D_IN_PROJ), lambda i: (0, 0), **const),
        ],
        out_specs=pl.BlockSpec((IN_TM, D_IN_PROJ), lambda i: (i, 0)),
        out_shape=jax.ShapeDtypeStruct((n_tok, D_IN_PROJ), BF16),
        scratch_shapes=[pltpu.VMEM((IN_TM, D_MODEL), BF16)],
        compiler_params=pltpu.CompilerParams(
            dimension_semantics=("arbitrary",), vmem_limit_bytes=VMEM_LIMIT),
        name="in_proj",
    )(x2, g_pre, w_in_p, bias_p)


MIX_TB = 256
MIX_NB = MIX_TB // CHUNK
STACK = PAIRS_PER_KV * CHUNK


def _silu(z):
    return z * (1.0 / (1.0 + jnp.exp(-z)))


def _mixers_kernel(sink_ref, u_ref, vg_ref, za_ref, q_ref, zb_ref, kv_ref, cos_ref, sin_ref,
                   lng_ref, lnb_ref, wsp_ref, bsp_ref, o_ref, q_s, k_s, v_s):
    j = pl.program_id(1)
    lane = lax.broadcasted_iota(jnp.int32, (CHUNK, LANES), 1)
    low_half = lane < HEAD_DIM
    even_group = (lane // HALF) % 2 == 0
    sin_sign = jnp.where(low_half, -1.0, 1.0).astype(F32)

    @pl.when(j == 0)
    def _():
        k_s[:, 0:CHUNK, :] = jnp.zeros((4, CHUNK, LANES), BF16)
        v_s[:, 0:CHUNK, :] = jnp.zeros((4, CHUNK, LANES), BF16)

    wrow = lax.broadcasted_iota(jnp.int32, (CHUNK, CHUNK), 0)
    wcol = lax.broadcasted_iota(jnp.int32, (CHUNK, CHUNK), 1)
    w_causal = [jnp.where(wrow >= wcol, wsp_ref[g], 0.0).astype(BF16) for g in range(GROUPS)]

    for n in range(MIX_NB):
        rows = slice(n * CHUNK, (n + 1) * CHUNK)
        srows = slice((n + 1) * CHUNK, (n + 2) * CHUNK)
        cos = cos_ref[rows, :]
        sin = sin_ref[rows, :] * sin_sign
        for p in range(N_PAIRS):
            x = q_ref[rows, p * LANES:(p + 1) * LANES].astype(F32)
            r = x * cos + pltpu.roll(x, HEAD_DIM, 1) * sin
            q_s[n, p] = (r * (HEAD_DIM ** -0.5)).astype(BF16)
        k = kv_ref[rows, 0:LANES].astype(F32)
        kr = k * cos + pltpu.roll(k, HEAD_DIM, 1) * sin
        ke0 = jnp.where(even_group, kr, 0.0)
        ko1 = jnp.where(even_group, 0.0, kr)
        k_s[0, srows, :] = ke0.astype(BF16)
        k_s[1, srows, :] = pltpu.roll(ke0, HALF, 1).astype(BF16)
        k_s[2, srows, :] = pltpu.roll(ko1, LANES - HALF, 1).astype(BF16)
        k_s[3, srows, :] = ko1.astype(BF16)
        v = kv_ref[rows, LANES:2 * LANES].astype(F32)
        vr = pltpu.roll(v, HEAD_DIM, 1)
        v_s[0, srows, :] = jnp.where(low_half, v, 1.0).astype(BF16)
        v_s[1, srows, :] = jnp.where(low_half, 1.0, vr).astype(BF16)
        v_s[2, srows, :] = jnp.where(low_half, vr, 1.0).astype(BF16)
        v_s[3, srows, :] = jnp.where(low_half, 1.0, v).astype(BF16)

    srow = lax.broadcasted_iota(jnp.int32, (STACK, CHUNK), 0)
    scol = lax.broadcasted_iota(jnp.int32, (STACK, CHUNK), 1)
    upper = scol > (srow % CHUNK)
    low_half_s = scol < HEAD_DIM
    no_prev = jnp.where(j > 0, 0.0, -jnp.inf).astype(F32)

    for n in range(MIX_NB):
        rows = slice(n * CHUNK, (n + 1) * CHUNK)
        krows = slice(n * CHUNK, (n + 2) * CHUNK)

        v = vg_ref[rows, :].astype(F32)
        mu = jnp.mean(v, axis=-1, keepdims=True)
        vc = v - mu
        var = jnp.mean(vc * vc, axis=-1, keepdims=True)
        vn = (vc * lax.rsqrt(var + EPS) * lng_ref[...] + lnb_ref[...]).astype(BF16)
        for g in range(GROUPS):
            cols = slice(g * CHUNK, (g + 1) * CHUNK)
            mixed = jnp.dot(w_causal[g], vn[:, cols], preferred_element_type=F32)
            mixed = mixed + bsp_ref[:, cols]
            y = u_ref[rows, cols].astype(F32) * mixed
            y = y * _silu(za_ref[rows, cols].astype(F32))
            o_ref[rows, cols] = y.astype(BF16)

        for h in range(N_KV_HEADS):
            qh = q_s[n, h * PAIRS_PER_KV:(h + 1) * PAIRS_PER_KV].reshape(STACK, LANES)
            outs, sink_terms = [], []
            for parity in range(2):
                var_idx = 2 * h + parity
                kk = k_s[var_idx, krows, :]
                s = lax.dot_general(qh, kk, (((1,), (1,)), ((), ())),
                                    preferred_element_type=F32)
                s_prev = s[:, 0:CHUNK]
                if n == 0:
                    s_prev = s_prev + no_prev
                merged = jnp.where(upper, s_prev, s[:, CHUNK:2 * CHUNK])
                sink = jnp.concatenate(
                    [jnp.full((CHUNK, 1), sink_ref[h * 2 * PAIRS_PER_KV + 2 * p + parity], F32)
                     for p in range(PAIRS_PER_KV)], axis=0)
                m = jnp.maximum(jnp.max(merged, axis=-1, keepdims=True), sink)
                e = jnp.exp(merged - m)
                p_cat = jnp.concatenate(
                    [jnp.where(upper, e, 0.0), jnp.where(upper, 0.0, e)], axis=1).astype(BF16)
                outs.append(jnp.dot(p_cat, v_s[var_idx, krows, :],
                                    preferred_element_type=F32))
                sink_terms.append(jnp.exp(sink - m))
            o_even, o_odd = outs
            num = jnp.where(low_half_s, o_even, o_odd)
            den = (jnp.where(low_half_s, o_odd, o_even)
                   + jnp.where(low_half_s, sink_terms[1], sink_terms[0]))
            att = num * pltpu.roll(1.0 / den, HEAD_DIM, 1)
            for p in range(PAIRS_PER_KV):
                pair = h * PAIRS_PER_KV + p
                cols = slice(pair * LANES, (pair + 1) * LANES)
                y = att[p * CHUNK:(p + 1) * CHUNK, :] * _silu(zb_ref[rows, cols].astype(F32))
                o_ref[rows, D_GMLP + pair * LANES:D_GMLP + (pair + 1) * LANES] = y.astype(BF16)

    k_s[:, 0:CHUNK, :] = k_s[:, MIX_TB:MIX_TB + CHUNK, :]
    v_s[:, 0:CHUNK, :] = v_s[:, MIX_TB:MIX_TB + CHUNK, :]


def _mixers(proj, cos_t, sin_t, sinks, ln_g, ln_b, w_sp, b_sp_full, batch, seq):
    n_tok = batch * seq
    tiles = seq // MIX_TB
    wide = lambda off: pl.BlockSpec((MIX_TB, 1024), lambda b, j, s: (b * tiles + j, off // 1024))
    const2 = lambda shape: pl.BlockSpec(shape, lambda b, j, s: (0, 0))
    grid_spec = pltpu.PrefetchScalarGridSpec(
        num_scalar_prefetch=1,
        grid=(batch, tiles),
        in_specs=[
            wide(OFF_U), wide(OFF_VG), wide(OFF_ZA), wide(OFF_Q), wide(OFF_ZB),
            pl.BlockSpec((MIX_TB, 2 * LANES), lambda b, j, s: (b * tiles + j, OFF_KV // (2 * LANES))),
            pl.BlockSpec((MIX_TB, LANES), lambda b, j, s: (b * tiles + j, 0)),
            pl.BlockSpec((MIX_TB, LANES), lambda b, j, s: (b * tiles + j, 0)),
            const2((1, D_GMLP)), const2((1, D_GMLP)),
            pl.BlockSpec((GROUPS, CHUNK, CHUNK), lambda b, j, s: (0, 0, 0)),
            const2((CHUNK, D_GMLP)),
        ],
        out_specs=pl.BlockSpec((MIX_TB, D_GMLP + D_ATTN), lambda b, j, s: (b * tiles + j, 0)),
        scratch_shapes=[
            pltpu.VMEM((MIX_NB, N_PAIRS, CHUNK, LANES), BF16),
            pltpu.VMEM((4, MIX_TB + CHUNK, LANES), BF16),
            pltpu.VMEM((4, MIX_TB + CHUNK, LANES), BF16),
        ],
    )
    return pl.pallas_call(
        _mixers_kernel,
        grid_spec=grid_spec,
        out_shape=jax.ShapeDtypeStruct((n_tok, D_GMLP + D_ATTN), BF16),
        compiler_params=pltpu.CompilerParams(
            dimension_semantics=("arbitrary", "arbitrary"), vmem_limit_bytes=VMEM_LIMIT),
        name="mixers",
    )(sinks, proj, proj, proj, proj, proj, proj, cos_t, sin_t, ln_g, ln_b, w_sp, b_sp_full)


OUT_TM = 512
OUT_ROWS = 256


def _out_proj_kernel(y_ref, x_ref, w_ref, g_ref, o_ref):
    for r in range(OUT_TM // OUT_ROWS):
        rows = slice(r * OUT_ROWS, (r + 1) * OUT_ROWS)
        y = jnp.dot(y_ref[rows, :], w_ref[...], preferred_element_type=F32)
        yn = y * lax.rsqrt(jnp.mean(y * y, axis=-1, keepdims=True) + EPS)
        o_ref[rows, :] = x_ref[rows, :] + yn * g_ref[...]


def _out_proj(ycat, x2, w_out_b, g_post):
    n_tok = x2.shape[0]
    const = dict(pipeline_mode=pl.Buffered(1))
    return pl.pallas_call(
        _out_proj_kernel,
        grid=(n_tok // OUT_TM,),
        in_specs=[
            pl.BlockSpec((OUT_TM, D_MODEL), lambda i: (i, 0)),
            pl.BlockSpec((OUT_TM, D_MODEL), lambda i: (i, 0)),
            pl.BlockSpec((D_MODEL, D_MODEL), lambda i: (0, 0), **const),
            pl.BlockSpec((1, D_MODEL), lambda i: (0, 0), **const),
        ],
        out_specs=pl.BlockSpec((OUT_TM, D_MODEL), lambda i: (i, 0)),
        out_shape=jax.ShapeDtypeStruct((n_tok, D_MODEL), F32),
        compiler_params=pltpu.CompilerParams(
            dimension_semantics=("arbitrary",), vmem_limit_bytes=VMEM_LIMIT),
        name="out_proj",
    )(ycat, x2, w_out_b, g_post)


def kernel(x, positions, g_pre, w_in, b_qkv, ln_v_g, ln_v_b, w_spatial, b_spatial, attn_sinks,
           w_out, g_post):
    batch, seq, d_model = x.shape
    depth = w_in.shape[0]
    assert d_model == D_MODEL and seq % MIX_TB == 0 and (batch * seq) % IN_TM == 0
    assert w_in.shape[-1] == D_IN_PROJ

    cos_t, sin_t = _rope_tables(positions.reshape(-1))
    x2 = x.reshape(batch * seq, D_MODEL)
    for l in range(depth):
        w_in_p = _permute_proj_columns(w_in[l]).astype(BF16)
        bias = _permute_proj_columns(jnp.concatenate(
            [jnp.zeros((3 * D_GMLP,), F32), b_qkv[l].astype(F32),
             jnp.zeros((D_ATTN,), F32)])).reshape(1, D_IN_PROJ)
        proj = _in_proj(x2, g_pre[l].reshape(1, D_MODEL), w_in_p, bias)
        b_sp_full = jnp.repeat(jnp.transpose(b_spatial[l]), CHUNK, axis=1)
        ycat = _mixers(proj, cos_t, sin_t, attn_sinks[l], ln_v_g[l].reshape(1, D_GMLP),
                       ln_v_b[l].reshape(1, D_GMLP), w_spatial[l], b_sp_full, batch, seq)
        x2 = _out_proj(ycat, x2, w_out[l].astype(BF16), g_post[l].reshape(1, D_MODEL))
    return x2.reshape(batch, seq, D_MODEL)
```

```python
import jax
import jax.numpy as jnp
from jax import lax
from jax.experimental import pallas as pl
from jax.experimental.pallas import tpu as pltpu

F32 = jnp.float32
BF16 = jnp.bfloat16

D_MODEL = 2048
D_GMLP = 1024
D_ATTN = 1024
CHUNK = 128
GROUPS = 8
HEAD_DIM = 64
HALF = HEAD_DIM // 2
N_Q_HEADS = 16
N_KV_HEADS = 2
N_PAIRS = N_Q_HEADS // 2
PAIRS_PER_KV = N_PAIRS // N_KV_HEADS
D_KV = N_KV_HEADS * HEAD_DIM
D_IN_PROJ = 3 * D_GMLP + D_ATTN + 2 * D_KV + D_ATTN
ROPE_THETA = 10000.0
EPS = 1e-6
LANES = 128
LANE_GROUPS = LANES // HALF

OFF_U, OFF_VG, OFF_ZA, OFF_Q, OFF_ZB = 0, 1024, 2048, 3072, 4096
OFF_KV = 5120

VMEM_LIMIT = 56 * 1024 * 1024


def _rope_tables_kernel(pos_ref, invf_ref, cos_ref, sin_ref):
    ang = pos_ref[...].astype(F32) * invf_ref[...]
    cos_ref[...] = jnp.cos(ang)
    sin_ref[...] = jnp.sin(ang)


def _rope_tables(positions, tile):
    n_tok = positions.size
    quart = tile // LANE_GROUPS
    inv_freq = ROPE_THETA ** (-jnp.arange(HALF, dtype=F32) * (2.0 / HEAD_DIM))
    pos = positions.reshape(n_tok // tile, LANE_GROUPS, quart)
    pos_rep = jnp.repeat(jnp.swapaxes(pos, 1, 2).reshape(n_tok // LANE_GROUPS, LANE_GROUPS),
                         HALF, axis=1)
    invf = jnp.tile(inv_freq, LANE_GROUPS).reshape(1, LANES)
    return pl.pallas_call(
        _rope_tables_kernel,
        out_shape=(jax.ShapeDtypeStruct((n_tok // LANE_GROUPS, LANES), F32),) * 2,
        name="rope_tables",
    )(pos_rep, invf)


IN_TM = 512
IN_ROWS = 128
IN_TN = 512
_Q0 = 3 * D_GMLP
_KV0 = _Q0 + D_ATTN
_ZB0 = _KV0 + 2 * D_KV
IN_CHUNKS = (
    [(c, IN_TN, c, None) for c in range(0, _Q0, IN_TN)]
    + [(_Q0 + c, IN_TN, OFF_Q + c, c) for c in range(0, D_ATTN, IN_TN)]
    + [(_KV0, 2 * D_KV, OFF_KV, D_ATTN)]
    + [(_ZB0 + c, IN_TN, OFF_ZB + c, None) for c in range(0, D_ATTN, IN_TN)]
)


def _in_proj_kernel(x_ref, g_ref, w_ref, b_ref, o_ref, h_ref):
    def norm_rows(r, carry):
        rows = pl.ds(pl.multiple_of(r * IN_ROWS, IN_ROWS), IN_ROWS)
        x = x_ref[rows, :]
        y = x * lax.rsqrt(jnp.mean(x * x, axis=-1, keepdims=True) + EPS)
        h_ref[rows, :] = (y * g_ref[...]).astype(BF16)
        return carry

    lax.fori_loop(0, IN_TM // IN_ROWS, norm_rows, 0)
    for src, width, dst, bias_off in IN_CHUNKS:
        acc = jnp.dot(h_ref[...], w_ref[:, src:src + width], preferred_element_type=F32)
        if bias_off is not None:
            acc = acc + b_ref[:, bias_off:bias_off + width]
        o_ref[:, dst:dst + width] = acc.astype(BF16)


def _in_proj(x2, g_pre, w_in_b, b_qkv):
    n_tok = x2.shape[0]
    const = dict(pipeline_mode=pl.Buffered(1))
    return pl.pallas_call(
        _in_proj_kernel,
        grid=(n_tok // IN_TM,),
        in_specs=[
            pl.BlockSpec((IN_TM, D_MODEL), lambda i: (i, 0)),
            pl.BlockSpec((1, D_MODEL), lambda i: (0, 0), **const),
            pl.BlockSpec((D_MODEL, D_IN_PROJ), lambda i: (0, 0), **const),
            pl.BlockSpec((1, D_ATTN + 2 * D_KV), lambda i: (0, 0), **const),
        ],
        out_specs=pl.BlockSpec((IN_TM, D_IN_PROJ), lambda i: (i, 0)),
        out_shape=jax.ShapeDtypeStruct((n_tok, D_IN_PROJ), BF16),
        scratch_shapes=[pltpu.VMEM((IN_TM, D_MODEL), BF16)],
        compiler_params=pltpu.CompilerParams(
            dimension_semantics=("arbitrary",), vmem_limit_bytes=VMEM_LIMIT),
        name="in_proj",
    )(x2, g_pre, w_in_b, b_qkv)


MIX_TB = 256
MIX_NB = MIX_TB // CHUNK
MIX_QUART = MIX_TB // LANE_GROUPS
LOG2E = 1.4426950408889634


def _silu(z):
    return z * (1.0 / (1.0 + jnp.exp(-z)))


def _expand_table(t_ref, n):
    piece_rows = min(CHUNK, MIX_QUART)
    lane_group = lax.broadcasted_iota(jnp.int32, (piece_rows, LANES), 1) // HALF
    pieces = []
    for tok in range(n * CHUNK, (n + 1) * CHUNK, piece_rows):
        g, r = tok // MIX_QUART, tok % MIX_QUART
        t = jnp.where(lane_group == g, t_ref[r:r + piece_rows, :], 0.0)
        t = t + pltpu.roll(t, HALF, 1)
        pieces.append(t + pltpu.roll(t, 2 * HALF, 1))
    return pieces[0] if len(pieces) == 1 else jnp.concatenate(pieces, axis=0)


def _mixers_kernel(sink_ref, u_ref, vg_ref, za_ref, q_ref, zb_ref, kv_ref, cos_ref, sin_ref,
                   lng_ref, lnb_ref, wsp_ref, bsp_ref, o_ref, q_s, k_s, vt_s):
    j = pl.program_id(1)
    row = lax.broadcasted_iota(jnp.int32, (CHUNK, LANES), 0)
    lane = lax.broadcasted_iota(jnp.int32, (CHUNK, LANES), 1)
    lane_group = lane // HALF
    low_half = lane < HEAD_DIM
    first_half = lane_group % 2 == 0
    prev_in_window = row > lane
    prev_f = prev_in_window.astype(BF16)
    cur_f = (row <= lane).astype(BF16)

    @pl.when(j == 0)
    def _():
        k_s[:, 0:CHUNK, :] = jnp.zeros((4, CHUNK, LANES), BF16)
        vt_s[:, 0:CHUNK] = jnp.zeros((LANES, CHUNK), BF16)

    no_prev = jnp.where(j > 0, 0.0, -jnp.inf).astype(F32)

    w_causal = [jnp.where(row >= lane, wsp_ref[g], 0.0).astype(BF16) for g in range(GROUPS)]

    q_scale = LOG2E * HEAD_DIM ** -0.5
    for n in range(MIX_NB):
        rows = slice(n * CHUNK, (n + 1) * CHUNK)
        srows = slice((n + 1) * CHUNK, (n + 2) * CHUNK)
        cos = _expand_table(cos_ref, n)
        sin = _expand_table(sin_ref, n)
        sin_hi = jnp.where(first_half, -sin, 0.0)
        sin_lo = jnp.where(first_half, 0.0, sin)

        def rope(x, c, s_hi, s_lo):
            return (x * c + pltpu.roll(x, LANES - HALF, 1) * s_hi + pltpu.roll(x, HALF, 1) * s_lo)

        cq, sq_hi, sq_lo = cos * q_scale, sin_hi * q_scale, sin_lo * q_scale
        for p in range(N_PAIRS):
            x = q_ref[rows, p * LANES:(p + 1) * LANES].astype(F32)
            q_s[n, p] = rope(x, cq, sq_hi, sq_lo).astype(BF16)
        kr = rope(kv_ref[rows, 0:LANES].astype(F32), cos, sin_hi, sin_lo)
        ke0 = jnp.where(low_half, kr, 0.0)
        ko1 = jnp.where(low_half, 0.0, kr)
        k_s[0, srows, :] = ke0.astype(BF16)
        k_s[1, srows, :] = pltpu.roll(ke0, HEAD_DIM, 1).astype(BF16)
        k_s[2, srows, :] = pltpu.roll(ko1, HEAD_DIM, 1).astype(BF16)
        k_s[3, srows, :] = ko1.astype(BF16)
        v = kv_ref[rows, LANES:2 * LANES].astype(F32)
        vt_s[:, (n + 1) * CHUNK:(n + 2) * CHUNK] = v.T.astype(BF16)

    for n in range(MIX_NB):
        rows = slice(n * CHUNK, (n + 1) * CHUNK)
        krows = slice(n * CHUNK, (n + 2) * CHUNK)

        v = vg_ref[rows, :].astype(F32)
        mu = jnp.mean(v, axis=-1, keepdims=True)
        vc = v - mu
        var = jnp.mean(vc * vc, axis=-1, keepdims=True)
        vn = (vc * lax.rsqrt(var + EPS) * lng_ref[...] + lnb_ref[...]).astype(BF16)
        for g in range(GROUPS):
            cols = slice(g * CHUNK, (g + 1) * CHUNK)
            mixed = jnp.dot(w_causal[g], vn[:, cols], preferred_element_type=F32)
            mixed = mixed + bsp_ref[:, cols]
            y = u_ref[rows, cols].astype(F32) * mixed
            y = y * _silu(za_ref[rows, cols].astype(F32))
            o_ref[rows, cols] = y.astype(BF16)

        for h in range(N_KV_HEADS):
            probs = {}
            recip = {}
            for parity in range(2):
                kk = k_s[2 * h + parity, krows, :]
                for pp in range(PAIRS_PER_KV // 2):
                    pair0 = h * PAIRS_PER_KV + 2 * pp
                    q2 = q_s[n, pair0:pair0 + 2].reshape(2 * CHUNK, LANES)
                    st = lax.dot_general(kk, q2, (((1,), (1,)), ((), ())),
                                         preferred_element_type=F32)
                    for c in range(2):
                        head = 2 * (pair0 + c) + parity
                        s_prev = st[0:CHUNK, c * CHUNK:(c + 1) * CHUNK]
                        if n == 0:
                            s_prev = s_prev + no_prev
                        s = jnp.where(prev_in_window, s_prev,
                                      st[CHUNK:2 * CHUNK, c * CHUNK:(c + 1) * CHUNK])
                        sink = sink_ref[head] * LOG2E
                        m = jnp.maximum(jnp.max(s, axis=0, keepdims=True), sink)
                        e = jnp.exp2(s - m)
                        den = jnp.sum(e, axis=0, keepdims=True) + jnp.exp2(sink - m)
                        recip[head] = 1.0 / den
                        e16 = e.astype(BF16)
                        probs[head] = jnp.concatenate([e16 * prev_f, e16 * cur_f], axis=0)
            vt = vt_s[h * HEAD_DIM:(h + 1) * HEAD_DIM, n * CHUNK:(n + 2) * CHUNK]
            for p in range(PAIRS_PER_KV):
                pair = h * PAIRS_PER_KV + p
                he, ho = 2 * pair, 2 * pair + 1
                ot = jnp.dot(vt, jnp.concatenate([probs[he], probs[ho]], axis=1),
                             preferred_element_type=F32)
                yt = jnp.concatenate([ot[:, 0:CHUNK] * recip[he],
                                      ot[:, CHUNK:2 * CHUNK] * recip[ho]], axis=0)
                cols = slice(pair * LANES, (pair + 1) * LANES)
                y = yt.T * _silu(zb_ref[rows, cols].astype(F32))
                o_ref[rows, D_GMLP + pair * LANES:D_GMLP + (pair + 1) * LANES] = y.astype(BF16)

    k_s[:, 0:CHUNK, :] = k_s[:, MIX_TB:MIX_TB + CHUNK, :]
    vt_s[:, 0:CHUNK] = vt_s[:, MIX_TB:MIX_TB + CHUNK]


def _mixers(proj, cos_c, sin_c, sinks, ln_g, ln_b, w_sp, b_sp_full, batch, seq):
    n_tok = batch * seq
    tiles = seq // MIX_TB
    wide = lambda off: pl.BlockSpec((MIX_TB, 1024), lambda b, j, s: (b * tiles + j, off // 1024))
    const2 = lambda shape: pl.BlockSpec(shape, lambda b, j, s: (0, 0))
    table = pl.BlockSpec((MIX_QUART, LANES), lambda b, j, s: (b * tiles + j, 0))
    grid_spec = pltpu.PrefetchScalarGridSpec(
        num_scalar_prefetch=1,
        grid=(batch, tiles),
        in_specs=[
            wide(OFF_U), wide(OFF_VG), wide(OFF_ZA), wide(OFF_Q), wide(OFF_ZB),
            pl.BlockSpec((MIX_TB, 2 * LANES), lambda b, j, s: (b * tiles + j, OFF_KV // (2 * LANES))),
            table, table,
            const2((1, D_GMLP)), const2((1, D_GMLP)),
            pl.BlockSpec((GROUPS, CHUNK, CHUNK), lambda b, j, s: (0, 0, 0)),
            const2((CHUNK, D_GMLP)),
        ],
        out_specs=pl.BlockSpec((MIX_TB, D_GMLP + D_ATTN), lambda b, j, s: (b * tiles + j, 0)),
        scratch_shapes=[
            pltpu.VMEM((MIX_NB, N_PAIRS, CHUNK, LANES), BF16),
            pltpu.VMEM((4, MIX_TB + CHUNK, LANES), BF16),
            pltpu.VMEM((LANES, MIX_TB + CHUNK), BF16),
        ],
    )
    return pl.pallas_call(
        _mixers_kernel,
        grid_spec=grid_spec,
        out_shape=jax.ShapeDtypeStruct((n_tok, D_GMLP + D_ATTN), BF16),
        compiler_params=pltpu.CompilerParams(
            dimension_semantics=("arbitrary", "arbitrary"), vmem_limit_bytes=VMEM_LIMIT),
        name="mixers",
    )(sinks, proj, proj, proj, proj, proj, proj, cos_c, sin_c, ln_g, ln_b, w_sp, b_sp_full)


OUT_TM = 512
OUT_ROWS = 256


def _out_proj_kernel(y_ref, x_ref, w_ref, g_ref, o_ref):
    for r in range(OUT_TM // OUT_ROWS):
        rows = slice(r * OUT_ROWS, (r + 1) * OUT_ROWS)
        y = jnp.dot(y_ref[rows, :], w_ref[...], preferred_element_type=F32)
        yn = y * lax.rsqrt(jnp.mean(y * y, axis=-1, keepdims=True) + EPS)
        o_ref[rows, :] = x_ref[rows, :] + yn * g_ref[...]


def _out_proj(ycat, x2, w_out_b, g_post):
    n_tok = x2.shape[0]
    const = dict(pipeline_mode=pl.Buffered(1))
    return pl.pallas_call(
        _out_proj_kernel,
        grid=(n_tok // OUT_TM,),
        in_specs=[
            pl.BlockSpec((OUT_TM, D_MODEL), lambda i: (i, 0)),
            pl.BlockSpec((OUT_TM, D_MODEL), lambda i: (i, 0)),
            pl.BlockSpec((D_MODEL, D_MODEL), lambda i: (0, 0), **const),
            pl.BlockSpec((1, D_MODEL), lambda i: (0, 0), **const),
        ],
        out_specs=pl.BlockSpec((OUT_TM, D_MODEL), lambda i: (i, 0)),
        out_shape=jax.ShapeDtypeStruct((n_tok, D_MODEL), F32),
        compiler_params=pltpu.CompilerParams(
            dimension_semantics=("arbitrary",), vmem_limit_bytes=VMEM_LIMIT),
        name="out_proj",
    )(ycat, x2, w_out_b, g_post)


def kernel(x, positions, g_pre, w_in, b_qkv, ln_v_g, ln_v_b, w_spatial, b_spatial, attn_sinks,
           w_out, g_post):
    batch, seq, d_model = x.shape
    depth = w_in.shape[0]
    assert d_model == D_MODEL and seq % MIX_TB == 0 and (batch * seq) % IN_TM == 0
    assert w_in.shape[-1] == D_IN_PROJ

    cos_c, sin_c = _rope_tables(positions.reshape(-1), MIX_TB)
    x2 = x.reshape(batch * seq, D_MODEL)
    for l in range(depth):
        proj = _in_proj(x2, g_pre[l].reshape(1, D_MODEL), w_in[l].astype(BF16),
                        b_qkv[l].reshape(1, D_ATTN + 2 * D_KV))
        b_sp_full = jnp.repeat(jnp.transpose(b_spatial[l]), CHUNK, axis=1)
        ycat = _mixers(proj, cos_c, sin_c, attn_sinks[l], ln_v_g[l].reshape(1, D_GMLP),
                       ln_v_b[l].reshape(1, D_GMLP), w_spatial[l], b_sp_full, batch, seq)
        x2 = _out_proj(ycat, x2, w_out[l].astype(BF16), g_post[l].reshape(1, D_MODEL))
    return x2.reshape(batch, seq, D_MODEL)
```

```python
import jax
import jax.numpy as jnp
from jax import lax
from jax.experimental import pallas as pl
from jax.experimental.pallas import tpu as pltpu

F32 = jnp.float32
BF16 = jnp.bfloat16

D_MODEL = 2048
D_GMLP = 1024
D_ATTN = 1024
CHUNK = 128
GROUPS = 8
HEAD_DIM = 64
HALF = HEAD_DIM // 2
N_Q_HEADS = 16
N_KV_HEADS = 2
N_PAIRS = N_Q_HEADS // 2
PAIRS_PER_KV = N_PAIRS // N_KV_HEADS
D_KV = N_KV_HEADS * HEAD_DIM
D_QKV = D_ATTN + 2 * D_KV
D_IN_PROJ = 3 * D_GMLP + D_QKV + D_ATTN
ROPE_THETA = 10000.0
EPS = 1e-6
LOG2E = 1.4426950408889634
LANES = 128
LANE_GROUPS = LANES // HALF

COL_U, COL_VG, COL_ZA = 0, D_GMLP, 2 * D_GMLP
COL_Q = 3 * D_GMLP
COL_KV = COL_Q + D_ATTN
COL_ZB = COL_KV + 2 * D_KV

VMEM_LIMIT = 56 * 1024 * 1024


def _silu(z):
    return z * (1.0 / (1.0 + jnp.exp(-z)))


def _rope_tables_kernel(pos_ref, invf_ref, cos_ref, sin_ref):
    ang = pos_ref[...].astype(F32) * invf_ref[...]
    cos_ref[...] = jnp.cos(ang)
    sin_ref[...] = jnp.sin(ang)


def _rope_tables(positions, tile):
    n_tok = positions.size
    quart = tile // LANE_GROUPS
    inv_freq = ROPE_THETA ** (-jnp.arange(HALF, dtype=F32) * (2.0 / HEAD_DIM))
    pos = positions.reshape(n_tok // tile, LANE_GROUPS, quart)
    pos_rep = jnp.repeat(jnp.swapaxes(pos, 1, 2).reshape(n_tok // LANE_GROUPS, LANE_GROUPS),
                         HALF, axis=1)
    invf = jnp.tile(inv_freq, LANE_GROUPS).reshape(1, LANES)
    return pl.pallas_call(
        _rope_tables_kernel,
        out_shape=(jax.ShapeDtypeStruct((n_tok // LANE_GROUPS, LANES), F32),) * 2,
        name="rope_tables",
    )(pos_rep, invf)


IN_TM = 4 * CHUNK
IN_NB = IN_TM // CHUNK
IN_ROWS = 128
IN_TN = 512
assert IN_NB == LANE_GROUPS


def _expand_tables(t_ref):
    lane_group = lax.broadcasted_iota(jnp.int32, (CHUNK, LANES), 1) // HALF
    blocks = []
    for n in range(IN_NB):
        t = jnp.where(lane_group == n, t_ref[...], 0.0)
        t = t + pltpu.roll(t, HALF, 1)
        blocks.append(t + pltpu.roll(t, 2 * HALF, 1))
    return jnp.concatenate(blocks, axis=0)


def _rope(x, cos, sin_hi, sin_lo):
    return x * cos + pltpu.roll(x, LANES - HALF, 1) * sin_hi + pltpu.roll(x, HALF, 1) * sin_lo


def _in_proj_kernel(x_ref, g_ref, w_ref, b_ref, cos_ref, sin_ref, lng_ref, lnb_ref, wsp_ref, bsp_ref,
                    ya_ref, q_ref, szb_ref, kv_ref, h_ref):
    def norm_rows(r, carry):
        rows = pl.ds(pl.multiple_of(r * IN_ROWS, IN_ROWS), IN_ROWS)
        x = x_ref[rows, :]
        y = x * lax.rsqrt(jnp.mean(x * x, axis=-1, keepdims=True) + EPS)
        h_ref[rows, :] = (y * g_ref[...]).astype(BF16)
        return carry

    lax.fori_loop(0, IN_TM // IN_ROWS, norm_rows, 0)

    def proj(col, width):
        return jnp.dot(h_ref[...], w_ref[:, col:col + width], preferred_element_type=F32)

    vg = proj(COL_VG, D_GMLP)
    mu = jnp.mean(vg, axis=-1, keepdims=True)
    vc = vg - mu
    var = jnp.mean(vc * vc, axis=-1, keepdims=True)
    vn = (vc * lax.rsqrt(var + EPS) * lng_ref[...] + lnb_ref[...]).astype(BF16)
    wrow = lax.broadcasted_iota(jnp.int32, (CHUNK, CHUNK), 0)
    wcol = lax.broadcasted_iota(jnp.int32, (CHUNK, CHUNK), 1)
    w_causal = [jnp.where(wrow >= wcol, wsp_ref[g], 0.0).astype(BF16) for g in range(GROUPS)]
    for c in range(0, D_GMLP, IN_TN):
        u = proj(COL_U + c, IN_TN)
        sza = _silu(proj(COL_ZA + c, IN_TN))
        for n in range(IN_NB):
            rows = slice(n * CHUNK, (n + 1) * CHUNK)
            for g in range(c // CHUNK, (c + IN_TN) // CHUNK):
                cols = slice(g * CHUNK, (g + 1) * CHUNK)
                lcols = slice(g * CHUNK - c, (g + 1) * CHUNK - c)
                mixed = jnp.dot(w_causal[g], vn[rows, cols], preferred_element_type=F32)
                mixed = mixed + bsp_ref[:, cols]
                ya_ref[rows, cols] = (u[rows, lcols] * mixed * sza[rows, lcols]).astype(BF16)

    cos = _expand_tables(cos_ref)
    sin = _expand_tables(sin_ref)
    first_half = (lax.broadcasted_iota(jnp.int32, (IN_TM, LANES), 1) // HALF) % 2 == 0
    sin_hi = jnp.where(first_half, -sin, 0.0)
    sin_lo = jnp.where(first_half, 0.0, sin)
    q_scale = LOG2E * HEAD_DIM ** -0.5
    cq, sq_hi, sq_lo = cos * q_scale, sin_hi * q_scale, sin_lo * q_scale
    for c in range(0, D_ATTN, IN_TN):
        q = proj(COL_Q + c, IN_TN) + b_ref[:, c:c + IN_TN]
        for s in range(0, IN_TN, LANES):
            q_ref[:, c + s:c + s + LANES] = _rope(q[:, s:s + LANES], cq, sq_hi, sq_lo).astype(BF16)
    kv = proj(COL_KV, 2 * D_KV) + b_ref[:, D_ATTN:D_QKV]
    kv_ref[:, 0:D_KV] = _rope(kv[:, 0:D_KV], cos, sin_hi, sin_lo).astype(BF16)
    kv_ref[:, D_KV:2 * D_KV] = kv[:, D_KV:2 * D_KV].astype(BF16)
    for c in range(0, D_ATTN, IN_TN):
        szb_ref[:, c:c + IN_TN] = _silu(proj(COL_ZB + c, IN_TN)).astype(BF16)


def _in_proj(x2, g_pre, w_in_b, b_qkv, cos_c, sin_c, ln_g, ln_b, w_sp, b_sp_full):
    n_tok = x2.shape[0]
    const = dict(pipeline_mode=pl.Buffered(1))
    row_tile = lambda width: pl.BlockSpec((IN_TM, width), lambda i: (i, 0))
    whole = lambda shape: pl.BlockSpec(shape, lambda i: (0,) * len(shape), **const)
    return pl.pallas_call(
        _in_proj_kernel,
        grid=(n_tok // IN_TM,),
        in_specs=[
            row_tile(D_MODEL),
            whole((1, D_MODEL)),
            whole((D_MODEL, D_IN_PROJ)),
            whole((1, D_QKV)),
            pl.BlockSpec((IN_TM // LANE_GROUPS, LANES), lambda i: (i, 0)),
            pl.BlockSpec((IN_TM // LANE_GROUPS, LANES), lambda i: (i, 0)),
            whole((1, D_GMLP)), whole((1, D_GMLP)),
            whole((GROUPS, CHUNK, CHUNK)),
            whole((CHUNK, D_GMLP)),
        ],
        out_specs=[row_tile(D_GMLP), row_tile(D_ATTN), row_tile(D_ATTN), row_tile(2 * D_KV)],
        out_shape=[
            jax.ShapeDtypeStruct((n_tok, D_GMLP), BF16),
            jax.ShapeDtypeStruct((n_tok, D_ATTN), BF16),
            jax.ShapeDtypeStruct((n_tok, D_ATTN), BF16),
            jax.ShapeDtypeStruct((n_tok, 2 * D_KV), BF16),
        ],
        scratch_shapes=[pltpu.VMEM((IN_TM, D_MODEL), BF16)],
        compiler_params=pltpu.CompilerParams(
            dimension_semantics=("arbitrary",), vmem_limit_bytes=VMEM_LIMIT),
        name="in_proj",
    )(x2, g_pre, w_in_b, b_qkv, cos_c, sin_c, ln_g, ln_b, w_sp, b_sp_full)


ATT_TB = 512
ATT_NB = ATT_TB // CHUNK


def _attention_kernel(sink_ref, q_ref, szb_ref, kv_ref, o_ref, k_s, vt_s):
    j = pl.program_id(1)
    row = lax.broadcasted_iota(jnp.int32, (CHUNK, LANES), 0)
    lane = lax.broadcasted_iota(jnp.int32, (CHUNK, LANES), 1)
    low_half = lane < HEAD_DIM
    prev_in_window = row > lane
    prev_f = prev_in_window.astype(BF16)
    cur_f = (row <= lane).astype(BF16)

    @pl.when(j == 0)
    def _():
        k_s[:, 0:CHUNK, :] = jnp.zeros((4, CHUNK, LANES), BF16)
        vt_s[0] = jnp.zeros((LANES, CHUNK), BF16)

    no_prev = jnp.where(j > 0, 0.0, -jnp.inf).astype(F32)

    for n in range(ATT_NB):
        rows = slice(n * CHUNK, (n + 1) * CHUNK)
        srows = slice((n + 1) * CHUNK, (n + 2) * CHUNK)
        kr = kv_ref[rows, 0:LANES].astype(F32)
        ke0 = jnp.where(low_half, kr, 0.0)
        ko1 = jnp.where(low_half, 0.0, kr)
        k_s[0, srows, :] = ke0.astype(BF16)
        k_s[1, srows, :] = pltpu.roll(ke0, HEAD_DIM, 1).astype(BF16)
        k_s[2, srows, :] = pltpu.roll(ko1, HEAD_DIM, 1).astype(BF16)
        k_s[3, srows, :] = ko1.astype(BF16)
        v = kv_ref[rows, LANES:2 * LANES].astype(F32)
        vt_s[n + 1] = v.T.astype(BF16)

    def block(n, first):
        row0 = n * CHUNK if first else pl.multiple_of(n * CHUNK, CHUNK)
        rows = pl.ds(row0, CHUNK)
        krows = pl.ds(row0, 2 * CHUNK)
        vt_all = jnp.concatenate([vt_s[n], vt_s[n + 1]], axis=1)
        for h in range(N_KV_HEADS):
            probs = {}
            recip = {}
            for parity in range(2):
                kk = k_s[2 * h + parity, krows, :]
                for pp in range(PAIRS_PER_KV // 2):
                    pair0 = h * PAIRS_PER_KV + 2 * pp
                    q2 = jnp.concatenate(
                        [q_ref[rows, (pair0 + c) * LANES:(pair0 + c + 1) * LANES] for c in range(2)],
                        axis=0)
                    st = lax.dot_general(kk, q2, (((1,), (1,)), ((), ())),
                                         preferred_element_type=F32)
                    for c in range(2):
                        head = 2 * (pair0 + c) + parity
                        s_prev = st[0:CHUNK, c * CHUNK:(c + 1) * CHUNK]
                        if first:
                            s_prev = s_prev + no_prev
                        s = jnp.where(prev_in_window, s_prev,
                                      st[CHUNK:2 * CHUNK, c * CHUNK:(c + 1) * CHUNK])
                        sink = sink_ref[head] * LOG2E
                        m = jnp.maximum(jnp.max(s, axis=0, keepdims=True), sink)
                        e = jnp.exp2(s - m)
                        den = jnp.sum(e, axis=0, keepdims=True) + jnp.exp2(sink - m)
                        recip[head] = 1.0 / den
                        e16 = e.astype(BF16)
                        probs[head] = jnp.concatenate([e16 * prev_f, e16 * cur_f], axis=0)
            vt = vt_all[h * HEAD_DIM:(h + 1) * HEAD_DIM, :]
            for p in range(PAIRS_PER_KV):
                pair = h * PAIRS_PER_KV + p
                he, ho = 2 * pair, 2 * pair + 1
                ot = jnp.dot(vt, jnp.concatenate([probs[he], probs[ho]], axis=1),
                             preferred_element_type=F32)
                yt = jnp.concatenate([ot[:, 0:CHUNK] * recip[he],
                                      ot[:, CHUNK:2 * CHUNK] * recip[ho]], axis=0)
                cols = slice(pair * LANES, (pair + 1) * LANES)
                o_ref[rows, cols] = (yt.T * szb_ref[rows, cols].astype(F32)).astype(BF16)

    block(0, True)

    def later_block(n, carry):
        block(n, False)
        return carry

    lax.fori_loop(1, ATT_NB, later_block, 0)

    k_s[:, 0:CHUNK, :] = k_s[:, ATT_TB:ATT_TB + CHUNK, :]
    vt_s[0] = vt_s[ATT_NB]


def _attention(q_r, szb, kv, sinks, batch, seq):
    n_tok = batch * seq
    tiles = seq // ATT_TB
    row_tile = lambda width: pl.BlockSpec((ATT_TB, width), lambda b, j, s: (b * tiles + j, 0))
    grid_spec = pltpu.PrefetchScalarGridSpec(
        num_scalar_prefetch=1,
        grid=(batch, tiles),
        in_specs=[row_tile(D_ATTN), row_tile(D_ATTN), row_tile(2 * D_KV)],
        out_specs=row_tile(D_ATTN),
        scratch_shapes=[
            pltpu.VMEM((4, ATT_TB + CHUNK, LANES), BF16),
            pltpu.VMEM((ATT_NB + 1, LANES, CHUNK), BF16),
        ],
    )
    return pl.pallas_call(
        _attention_kernel,
        grid_spec=grid_spec,
        out_shape=jax.ShapeDtypeStruct((n_tok, D_ATTN), BF16),
        compiler_params=pltpu.CompilerParams(
            dimension_semantics=("arbitrary", "arbitrary"), vmem_limit_bytes=VMEM_LIMIT),
        name="attention",
    )(sinks, q_r, szb, kv)


OUT_TM = 512
OUT_ROWS = 256


def _out_proj_kernel(ya_ref, yb_ref, x_ref, w_ref, g_ref, o_ref):
    for r in range(OUT_TM // OUT_ROWS):
        rows = slice(r * OUT_ROWS, (r + 1) * OUT_ROWS)
        y = jnp.dot(ya_ref[rows, :], w_ref[0:D_GMLP, :], preferred_element_type=F32)
        y = y + jnp.dot(yb_ref[rows, :], w_ref[D_GMLP:D_GMLP + D_ATTN, :],
                        preferred_element_type=F32)
        yn = y * lax.rsqrt(jnp.mean(y * y, axis=-1, keepdims=True) + EPS)
        o_ref[rows, :] = x_ref[rows, :] + yn * g_ref[...]


def _out_proj(y_a, y_b, x2, w_out_b, g_post):
    n_tok = x2.shape[0]
    const = dict(pipeline_mode=pl.Buffered(1))
    return pl.pallas_call(
        _out_proj_kernel,
        grid=(n_tok // OUT_TM,),
        in_specs=[
            pl.BlockSpec((OUT_TM, D_GMLP), lambda i: (i, 0)),
            pl.BlockSpec((OUT_TM, D_ATTN), lambda i: (i, 0)),
            pl.BlockSpec((OUT_TM, D_MODEL), lambda i: (i, 0)),
            pl.BlockSpec((D_GMLP + D_ATTN, D_MODEL), lambda i: (0, 0), **const),
            pl.BlockSpec((1, D_MODEL), lambda i: (0, 0), **const),
        ],
        out_specs=pl.BlockSpec((OUT_TM, D_MODEL), lambda i: (i, 0)),
        out_shape=jax.ShapeDtypeStruct((n_tok, D_MODEL), F32),
        compiler_params=pltpu.CompilerParams(
            dimension_semantics=("arbitrary",), vmem_limit_bytes=VMEM_LIMIT),
        name="out_proj",
    )(y_a, y_b, x2, w_out_b, g_post)


def kernel(x, positions, g_pre, w_in, b_qkv, ln_v_g, ln_v_b, w_spatial, b_spatial, attn_sinks,
           w_out, g_post):
    batch, seq, d_model = x.shape
    depth = w_in.shape[0]
    assert d_model == D_MODEL and seq % ATT_TB == 0 and seq % IN_TM == 0
    assert (batch * seq) % OUT_TM == 0 and w_in.shape[-1] == D_IN_PROJ

    cos_c, sin_c = _rope_tables(positions.reshape(-1), IN_TM)
    x2 = x.reshape(batch * seq, D_MODEL)
    for l in range(depth):
        b_sp_full = jnp.repeat(jnp.transpose(b_spatial[l]), CHUNK, axis=1)
        y_a, q_r, szb, kv = _in_proj(
            x2, g_pre[l].reshape(1, D_MODEL), w_in[l].astype(BF16), b_qkv[l].reshape(1, D_QKV),
            cos_c, sin_c, ln_v_g[l].reshape(1, D_GMLP), ln_v_b[l].reshape(1, D_GMLP),
            w_spatial[l], b_sp_full)
        y_b = _attention(q_r, szb, kv, attn_sinks[l], batch, seq)
        x2 = _out_proj(y_a, y_b, x2, w_out[l].astype(BF16), g_post[l].reshape(1, D_MODEL))
    return x2.reshape(batch, seq, D_MODEL)
```

```python
import jax
import jax.numpy as jnp
from jax import lax
from jax.experimental import pallas as pl
from jax.experimental.pallas import tpu as pltpu

F32 = jnp.float32
BF16 = jnp.bfloat16

D_MODEL = 2048
D_GMLP = 1024
D_ATTN = 1024
CHUNK = 128
GROUPS = 8
HEAD_DIM = 64
HALF = HEAD_DIM // 2
N_Q_HEADS = 16
N_KV_HEADS = 2
N_PAIRS = N_Q_HEADS // 2
PAIRS_PER_KV = N_PAIRS // N_KV_HEADS
D_KV = N_KV_HEADS * HEAD_DIM
D_QKV = D_ATTN + 2 * D_KV
D_IN_PROJ = 3 * D_GMLP + D_QKV + D_ATTN
ROPE_THETA = 10000.0
EPS = 1e-6
LOG2E = 1.4426950408889634
LANES = 128
LANE_GROUPS = LANES // HALF

COL_U, COL_VG, COL_ZA = 0, D_GMLP, 2 * D_GMLP
COL_Q = 3 * D_GMLP
COL_KV = COL_Q + D_ATTN
COL_ZB = COL_KV + 2 * D_KV

VMEM_LIMIT = 56 * 1024 * 1024


def _silu(z):
    return z * (1.0 / (1.0 + jnp.exp(-z)))


def _rope_tables_kernel(pos_ref, invf_ref, cos_ref, sin_ref):
    ang = pos_ref[...].astype(F32) * invf_ref[...]
    cos_ref[...] = jnp.cos(ang)
    sin_ref[...] = jnp.sin(ang)


def _rope_tables(positions, tile):
    n_tok = positions.size
    quart = tile // LANE_GROUPS
    inv_freq = ROPE_THETA ** (-jnp.arange(HALF, dtype=F32) * (2.0 / HEAD_DIM))
    pos = positions.reshape(n_tok // tile, LANE_GROUPS, quart)
    pos_rep = jnp.repeat(jnp.swapaxes(pos, 1, 2).reshape(n_tok // LANE_GROUPS, LANE_GROUPS),
                         HALF, axis=1)
    invf = jnp.tile(inv_freq, LANE_GROUPS).reshape(1, LANES)
    return pl.pallas_call(
        _rope_tables_kernel,
        out_shape=(jax.ShapeDtypeStruct((n_tok // LANE_GROUPS, LANES), F32),) * 2,
        name="rope_tables",
    )(pos_rep, invf)


IN_TM = 4 * CHUNK
IN_NB = IN_TM // CHUNK
IN_ROWS = 128
IN_TN = 512
assert IN_NB == LANE_GROUPS


def _expand_tables(t_ref):
    lane_group = lax.broadcasted_iota(jnp.int32, (CHUNK, LANES), 1) // HALF
    blocks = []
    for n in range(IN_NB):
        t = jnp.where(lane_group == n, t_ref[...], 0.0)
        t = t + pltpu.roll(t, HALF, 1)
        blocks.append(t + pltpu.roll(t, 2 * HALF, 1))
    return jnp.concatenate(blocks, axis=0)


def _rope(x, cos, sin_hi, sin_lo):
    return x * cos + pltpu.roll(x, LANES - HALF, 1) * sin_hi + pltpu.roll(x, HALF, 1) * sin_lo


def _in_proj_kernel(x_ref, g_ref, w_ref, b_ref, cos_ref, sin_ref, lng_ref, lnb_ref, wsp_ref, bsp_ref,
                    ya_ref, qt_ref, szb_ref, kv_ref, h_ref):
    def norm_rows(r, carry):
        rows = pl.ds(pl.multiple_of(r * IN_ROWS, IN_ROWS), IN_ROWS)
        x = x_ref[rows, :]
        y = x * lax.rsqrt(jnp.mean(x * x, axis=-1, keepdims=True) + EPS)
        h_ref[rows, :] = (y * g_ref[...]).astype(BF16)
        return carry

    lax.fori_loop(0, IN_TM // IN_ROWS, norm_rows, 0)

    def proj(col, width):
        return jnp.dot(h_ref[...], w_ref[:, col:col + width], preferred_element_type=F32)

    vg = proj(COL_VG, D_GMLP)
    mu = jnp.mean(vg, axis=-1, keepdims=True)
    vc = vg - mu
    var = jnp.mean(vc * vc, axis=-1, keepdims=True)
    vn = (vc * lax.rsqrt(var + EPS) * lng_ref[...] + lnb_ref[...]).astype(BF16)
    wrow = lax.broadcasted_iota(jnp.int32, (CHUNK, CHUNK), 0)
    wcol = lax.broadcasted_iota(jnp.int32, (CHUNK, CHUNK), 1)
    w_causal = [jnp.where(wrow >= wcol, wsp_ref[g], 0.0).astype(BF16) for g in range(GROUPS)]
    for c in range(0, D_GMLP, IN_TN):
        u = proj(COL_U + c, IN_TN)
        sza = _silu(proj(COL_ZA + c, IN_TN))
        for n in range(IN_NB):
            rows = slice(n * CHUNK, (n + 1) * CHUNK)
            for g in range(c // CHUNK, (c + IN_TN) // CHUNK):
                cols = slice(g * CHUNK, (g + 1) * CHUNK)
                lcols = slice(g * CHUNK - c, (g + 1) * CHUNK - c)
                mixed = jnp.dot(w_causal[g], vn[rows, cols], preferred_element_type=F32)
                mixed = mixed + bsp_ref[:, cols]
                ya_ref[rows, cols] = (u[rows, lcols] * mixed * sza[rows, lcols]).astype(BF16)

    cos = _expand_tables(cos_ref)
    sin = _expand_tables(sin_ref)
    first_half = (lax.broadcasted_iota(jnp.int32, (IN_TM, LANES), 1) // HALF) % 2 == 0
    sin_hi = jnp.where(first_half, -sin, 0.0)
    sin_lo = jnp.where(first_half, 0.0, sin)
    q_scale = LOG2E * HEAD_DIM ** -0.5
    cq, sq_hi, sq_lo = cos * q_scale, sin_hi * q_scale, sin_lo * q_scale
    for c in range(0, D_ATTN, IN_TN):
        q = proj(COL_Q + c, IN_TN) + b_ref[:, c:c + IN_TN]
        for s in range(0, IN_TN, LANES):
            roped = _rope(q[:, s:s + LANES], cq, sq_hi, sq_lo)
            for n in range(IN_NB):
                qt_ref[(c + s) // LANES, n] = roped[n * CHUNK:(n + 1) * CHUNK, :].T.astype(BF16)
    kv = proj(COL_KV, 2 * D_KV) + b_ref[:, D_ATTN:D_QKV]
    kv_ref[:, 0:D_KV] = _rope(kv[:, 0:D_KV], cos, sin_hi, sin_lo).astype(BF16)
    kv_ref[:, D_KV:2 * D_KV] = kv[:, D_KV:2 * D_KV].astype(BF16)
    for c in range(0, D_ATTN, IN_TN):
        szb_ref[:, c:c + IN_TN] = _silu(proj(COL_ZB + c, IN_TN)).astype(BF16)


def _in_proj(x2, g_pre, w_in_b, b_qkv, cos_c, sin_c, ln_g, ln_b, w_sp, b_sp_full):
    n_tok = x2.shape[0]
    const = dict(pipeline_mode=pl.Buffered(1))
    row_tile = lambda width: pl.BlockSpec((IN_TM, width), lambda i: (i, 0))
    whole = lambda shape: pl.BlockSpec(shape, lambda i: (0,) * len(shape), **const)
    return pl.pallas_call(
        _in_proj_kernel,
        grid=(n_tok // IN_TM,),
        in_specs=[
            row_tile(D_MODEL),
            whole((1, D_MODEL)),
            whole((D_MODEL, D_IN_PROJ)),
            whole((1, D_QKV)),
            pl.BlockSpec((IN_TM // LANE_GROUPS, LANES), lambda i: (i, 0)),
            pl.BlockSpec((IN_TM // LANE_GROUPS, LANES), lambda i: (i, 0)),
            whole((1, D_GMLP)), whole((1, D_GMLP)),
            whole((GROUPS, CHUNK, CHUNK)),
            whole((CHUNK, D_GMLP)),
        ],
        out_specs=[
            row_tile(D_GMLP),
            pl.BlockSpec((N_PAIRS, IN_NB, LANES, CHUNK), lambda i: (0, i, 0, 0)),
            row_tile(D_ATTN), row_tile(2 * D_KV)],
        out_shape=[
            jax.ShapeDtypeStruct((n_tok, D_GMLP), BF16),
            jax.ShapeDtypeStruct((N_PAIRS, n_tok // CHUNK, LANES, CHUNK), BF16),
            jax.ShapeDtypeStruct((n_tok, D_ATTN), BF16),
            jax.ShapeDtypeStruct((n_tok, 2 * D_KV), BF16),
        ],
        scratch_shapes=[pltpu.VMEM((IN_TM, D_MODEL), BF16)],
        compiler_params=pltpu.CompilerParams(
            dimension_semantics=("arbitrary",), vmem_limit_bytes=VMEM_LIMIT),
        name="in_proj",
    )(x2, g_pre, w_in_b, b_qkv, cos_c, sin_c, ln_g, ln_b, w_sp, b_sp_full)


ATT_TB = 512
ATT_NB = ATT_TB // CHUNK


def _attention_kernel(sink_ref, qt_ref, szb_ref, kv_ref, o_ref, k_s, vt_s):
    j = pl.program_id(1)
    row = lax.broadcasted_iota(jnp.int32, (CHUNK, LANES), 0)
    lane = lax.broadcasted_iota(jnp.int32, (CHUNK, LANES), 1)
    low_half = lane < HEAD_DIM
    prev_in_window = row > lane
    prev_f = prev_in_window.astype(BF16)
    cur_f = (row <= lane).astype(BF16)

    @pl.when(j == 0)
    def _():
        k_s[:, 0:CHUNK, :] = jnp.zeros((4, CHUNK, LANES), BF16)
        vt_s[0] = jnp.zeros((LANES, CHUNK), BF16)

    no_prev = jnp.where(j > 0, 0.0, -jnp.inf).astype(F32)

    for n in range(ATT_NB):
        rows = slice(n * CHUNK, (n + 1) * CHUNK)
        srows = slice((n + 1) * CHUNK, (n + 2) * CHUNK)
        kr = kv_ref[rows, 0:LANES].astype(F32)
        ke0 = jnp.where(low_half, kr, 0.0)
        ko1 = jnp.where(low_half, 0.0, kr)
        k_s[0, srows, :] = ke0.astype(BF16)
        k_s[1, srows, :] = pltpu.roll(ke0, HEAD_DIM, 1).astype(BF16)
        k_s[2, srows, :] = pltpu.roll(ko1, HEAD_DIM, 1).astype(BF16)
        k_s[3, srows, :] = ko1.astype(BF16)
        v = kv_ref[rows, LANES:2 * LANES].astype(F32)
        vt_s[n + 1] = v.T.astype(BF16)

    groups = [(n, h, pp) for n in range(ATT_NB) for h in range(N_KV_HEADS)
              for pp in range(PAIRS_PER_KV // 2)]

    def scores(n, h, pp):
        krows = slice(n * CHUNK, (n + 2) * CHUNK)
        kk = jnp.concatenate([k_s[2 * h, krows, :], k_s[2 * h + 1, krows, :]], axis=0)
        pair0 = h * PAIRS_PER_KV + 2 * pp
        q2t = jnp.concatenate([qt_ref[pair0, n], qt_ref[pair0 + 1, n]], axis=1)
        return jnp.dot(kk, q2t, preferred_element_type=F32)

    def finish(st, n, h, pp):
        rows = slice(n * CHUNK, (n + 1) * CHUNK)
        pair0 = h * PAIRS_PER_KV + 2 * pp
        vt = jnp.concatenate([vt_s[n], vt_s[n + 1]], axis=1)[h * HEAD_DIM:(h + 1) * HEAD_DIM, :]
        for c in range(2):
            probs, recip = [], []
            for parity in range(2):
                k0 = parity * 2 * CHUNK
                s_prev = st[k0:k0 + CHUNK, c * CHUNK:(c + 1) * CHUNK]
                if n == 0:
                    s_prev = s_prev + no_prev
                s = jnp.where(prev_in_window, s_prev,
                              st[k0 + CHUNK:k0 + 2 * CHUNK, c * CHUNK:(c + 1) * CHUNK])
                sink = sink_ref[2 * (pair0 + c) + parity] * LOG2E
                m = jnp.maximum(jnp.max(s, axis=0, keepdims=True), sink)
                e = jnp.exp2(s - m)
                den = jnp.sum(e, axis=0, keepdims=True) + jnp.exp2(sink - m)
                recip.append(1.0 / den)
                e16 = e.astype(BF16)
                probs.append(jnp.concatenate([e16 * prev_f, e16 * cur_f], axis=0))
            ot = jnp.dot(vt, jnp.concatenate(probs, axis=1), preferred_element_type=F32)
            yt = jnp.concatenate([ot[:, 0:CHUNK] * recip[0], ot[:, CHUNK:2 * CHUNK] * recip[1]],
                                 axis=0)
            cols = slice((pair0 + c) * LANES, (pair0 + c + 1) * LANES)
            o_ref[rows, cols] = (yt.T * szb_ref[rows, cols].astype(F32)).astype(BF16)

    st = scores(*groups[0])
    for i in range(1, len(groups) + 1):
        st_next = scores(*groups[i]) if i < len(groups) else None
        finish(st, *groups[i - 1])
        st = st_next

    k_s[:, 0:CHUNK, :] = k_s[:, ATT_TB:ATT_TB + CHUNK, :]
    vt_s[0] = vt_s[ATT_NB]


def _attention(q_t, szb, kv, sinks, batch, seq):
    n_tok = batch * seq
    tiles = seq // ATT_TB
    row_tile = lambda width: pl.BlockSpec((ATT_TB, width), lambda b, j, s: (b * tiles + j, 0))
    grid_spec = pltpu.PrefetchScalarGridSpec(
        num_scalar_prefetch=1,
        grid=(batch, tiles),
        in_specs=[
            pl.BlockSpec((N_PAIRS, ATT_NB, LANES, CHUNK), lambda b, j, s: (0, b * tiles + j, 0, 0)),
            row_tile(D_ATTN), row_tile(2 * D_KV)],
        out_specs=row_tile(D_ATTN),
        scratch_shapes=[
            pltpu.VMEM((4, ATT_TB + CHUNK, LANES), BF16),
            pltpu.VMEM((ATT_NB + 1, LANES, CHUNK), BF16),
        ],
    )
    return pl.pallas_call(
        _attention_kernel,
        grid_spec=grid_spec,
        out_shape=jax.ShapeDtypeStruct((n_tok, D_ATTN), BF16),
        compiler_params=pltpu.CompilerParams(
            dimension_semantics=("arbitrary", "arbitrary"), vmem_limit_bytes=VMEM_LIMIT),
        name="attention",
    )(sinks, q_t, szb, kv)


OUT_TM = 512
OUT_ROWS = 256


def _out_proj_kernel(ya_ref, yb_ref, x_ref, w_ref, g_ref, o_ref):
    for r in range(OUT_TM // OUT_ROWS):
        rows = slice(r * OUT_ROWS, (r + 1) * OUT_ROWS)
        y = jnp.dot(ya_ref[rows, :], w_ref[0:D_GMLP, :], preferred_element_type=F32)
        y = y + jnp.dot(yb_ref[rows, :], w_ref[D_GMLP:D_GMLP + D_ATTN, :],
                        preferred_element_type=F32)
        yn = y * lax.rsqrt(jnp.mean(y * y, axis=-1, keepdims=True) + EPS)
        o_ref[rows, :] = x_ref[rows, :] + yn * g_ref[...]


def _out_proj(y_a, y_b, x2, w_out_b, g_post):
    n_tok = x2.shape[0]
    const = dict(pipeline_mode=pl.Buffered(1))
    return pl.pallas_call(
        _out_proj_kernel,
        grid=(n_tok // OUT_TM,),
        in_specs=[
            pl.BlockSpec((OUT_TM, D_GMLP), lambda i: (i, 0)),
            pl.BlockSpec((OUT_TM, D_ATTN), lambda i: (i, 0)),
            pl.BlockSpec((OUT_TM, D_MODEL), lambda i: (i, 0)),
            pl.BlockSpec((D_GMLP + D_ATTN, D_MODEL), lambda i: (0, 0), **const),
            pl.BlockSpec((1, D_MODEL), lambda i: (0, 0), **const),
        ],
        out_specs=pl.BlockSpec((OUT_TM, D_MODEL), lambda i: (i, 0)),
        out_shape=jax.ShapeDtypeStruct((n_tok, D_MODEL), F32),
        compiler_params=pltpu.CompilerParams(
            dimension_semantics=("arbitrary",), vmem_limit_bytes=VMEM_LIMIT),
        name="out_proj",
    )(y_a, y_b, x2, w_out_b, g_post)


def kernel(x, positions, g_pre, w_in, b_qkv, ln_v_g, ln_v_b, w_spatial, b_spatial, attn_sinks,
           w_out, g_post):
    batch, seq, d_model = x.shape
    depth = w_in.shape[0]
    assert d_model == D_MODEL and seq % ATT_TB == 0 and seq % IN_TM == 0
    assert (batch * seq) % OUT_TM == 0 and w_in.shape[-1] == D_IN_PROJ

    cos_c, sin_c = _rope_tables(positions.reshape(-1), IN_TM)
    x2 = x.reshape(batch * seq, D_MODEL)
    for l in range(depth):
        b_sp_full = jnp.repeat(jnp.transpose(b_spatial[l]), CHUNK, axis=1)
        y_a, q_t, szb, kv = _in_proj(
            x2, g_pre[l].reshape(1, D_MODEL), w_in[l].astype(BF16), b_qkv[l].reshape(1, D_QKV),
            cos_c, sin_c, ln_v_g[l].reshape(1, D_GMLP), ln_v_b[l].reshape(1, D_GMLP),
            w_spatial[l], b_sp_full)
        y_b = _attention(q_t, szb, kv, attn_sinks[l], batch, seq)
        x2 = _out_proj(y_a, y_b, x2, w_out[l].astype(BF16), g_post[l].reshape(1, D_MODEL))
    return x2.reshape(batch, seq, D_MODEL)
```

```python
import jax
import jax.numpy as jnp
from jax import lax
from jax.experimental import pallas as pl
from jax.experimental.pallas import tpu as pltpu

F32 = jnp.float32
BF16 = jnp.bfloat16

D_MODEL = 2048
D_GMLP = 1024
D_ATTN = 1024
CHUNK = 128
GROUPS = 8
HEAD_DIM = 64
HALF = HEAD_DIM // 2
N_Q_HEADS = 16
N_KV_HEADS = 2
N_PAIRS = N_Q_HEADS // 2
PAIRS_PER_KV = N_PAIRS // N_KV_HEADS
D_KV = N_KV_HEADS * HEAD_DIM
D_QKV = D_ATTN + 2 * D_KV
D_IN_PROJ = 3 * D_GMLP + D_QKV + D_ATTN
ROPE_THETA = 10000.0
EPS = 1e-6
LOG2E = 1.4426950408889634
LANES = 128
LANE_GROUPS = LANES // HALF

COL_U, COL_VG, COL_ZA = 0, D_GMLP, 2 * D_GMLP
COL_Q = 3 * D_GMLP
COL_KV = COL_Q + D_ATTN
COL_ZB = COL_KV + 2 * D_KV

VMEM_LIMIT = 56 * 1024 * 1024


def _silu(z):
    return z * (1.0 / (1.0 + jnp.exp(-z)))


IN_TM = 4 * CHUNK
IN_NB = IN_TM // CHUNK
IN_ROWS = 128
IN_TN = 512
assert IN_NB == LANE_GROUPS


def _rope_positions(positions):
    n_tok = positions.size
    pos = positions.reshape(n_tok // IN_TM, LANE_GROUPS, CHUNK)
    return jnp.repeat(jnp.swapaxes(pos, 1, 2).reshape(n_tok // LANE_GROUPS, LANE_GROUPS),
                      HALF, axis=1)


def _expand_table(t):
    lane_group = lax.broadcasted_iota(jnp.int32, (CHUNK, LANES), 1) // HALF
    blocks = []
    for n in range(IN_NB):
        b = jnp.where(lane_group == n, t, 0.0)
        b = b + pltpu.roll(b, HALF, 1)
        blocks.append(b + pltpu.roll(b, 2 * HALF, 1))
    return jnp.concatenate(blocks, axis=0)


def _rope(x, cos, sin_hi, sin_lo):
    return x * cos + pltpu.roll(x, LANES - HALF, 1) * sin_hi + pltpu.roll(x, HALF, 1) * sin_lo


def _in_proj_kernel(x_ref, g_ref, w_ref, b_ref, pos_ref, invf_ref, lng_ref, lnb_ref, wsp_ref, bsp_ref,
                    ya_ref, qt_ref, szb_ref, kv_ref, h0_ref, h1_ref):
    i = pl.program_id(0)

    def normalize(h_ref):
        for r in range(0, IN_TM, IN_ROWS):
            x = x_ref[r:r + IN_ROWS, :]
            y = x * lax.rsqrt(jnp.mean(x * x, axis=-1, keepdims=True) + EPS)
            h_ref[r:r + IN_ROWS, :] = (y * g_ref[...]).astype(BF16)

    def project(h_ref):
        def proj(col, width):
            return jnp.dot(h_ref[...], w_ref[:, col:col + width], preferred_element_type=F32)

        vg = proj(COL_VG, D_GMLP)
        mu = jnp.mean(vg, axis=-1, keepdims=True)
        vc = vg - mu
        var = jnp.mean(vc * vc, axis=-1, keepdims=True)
        vn = (vc * lax.rsqrt(var + EPS) * lng_ref[...] + lnb_ref[...]).astype(BF16)
        wrow = lax.broadcasted_iota(jnp.int32, (CHUNK, CHUNK), 0)
        wcol = lax.broadcasted_iota(jnp.int32, (CHUNK, CHUNK), 1)
        w_causal = [jnp.where(wrow >= wcol, wsp_ref[g], 0.0).astype(BF16) for g in range(GROUPS)]
        for c in range(0, D_GMLP, IN_TN):
            u = proj(COL_U + c, IN_TN)
            sza = _silu(proj(COL_ZA + c, IN_TN))
            for n in range(IN_NB):
                rows = slice(n * CHUNK, (n + 1) * CHUNK)
                for g in range(c // CHUNK, (c + IN_TN) // CHUNK):
                    cols = slice(g * CHUNK, (g + 1) * CHUNK)
                    lcols = slice(g * CHUNK - c, (g + 1) * CHUNK - c)
                    mixed = jnp.dot(w_causal[g], vn[rows, cols], preferred_element_type=F32)
                    mixed = mixed + bsp_ref[:, cols]
                    ya_ref[rows, cols] = (u[rows, lcols] * mixed * sza[rows, lcols]).astype(BF16)

        ang = pos_ref[...].astype(F32) * invf_ref[...]
        cos = _expand_table(jnp.cos(ang))
        sin = _expand_table(jnp.sin(ang))
        first_half = (lax.broadcasted_iota(jnp.int32, (IN_TM, LANES), 1) // HALF) % 2 == 0
        sin_hi = jnp.where(first_half, -sin, 0.0)
        sin_lo = jnp.where(first_half, 0.0, sin)
        q_scale = LOG2E * HEAD_DIM ** -0.5
        cq, sq_hi, sq_lo = cos * q_scale, sin_hi * q_scale, sin_lo * q_scale
        for c in range(0, D_ATTN, IN_TN):
            q = proj(COL_Q + c, IN_TN) + b_ref[:, c:c + IN_TN]
            for s in range(0, IN_TN, LANES):
                roped = _rope(q[:, s:s + LANES], cq, sq_hi, sq_lo)
                for n in range(IN_NB):
                    qt_ref[(c + s) // LANES, n] = roped[n * CHUNK:(n + 1) * CHUNK, :].T.astype(BF16)
        kv = proj(COL_KV, 2 * D_KV) + b_ref[:, D_ATTN:D_QKV]
        kv_ref[:, 0:D_KV] = _rope(kv[:, 0:D_KV], cos, sin_hi, sin_lo).astype(BF16)
        kv_ref[:, D_KV:2 * D_KV] = kv[:, D_KV:2 * D_KV].astype(BF16)
        for c in range(0, D_ATTN, IN_TN):
            szb_ref[:, c:c + IN_TN] = _silu(proj(COL_ZB + c, IN_TN)).astype(BF16)

    @pl.when(i == 0)
    def _():
        normalize(h0_ref)

    @pl.when(i % 2 == 1)
    def _():
        normalize(h1_ref)
        project(h0_ref)

    @pl.when((i > 0) & (i % 2 == 0))
    def _():
        normalize(h0_ref)
        project(h1_ref)


def _in_proj(x2, g_pre, w_in_b, b_qkv, pos_rep, invf, ln_g, ln_b, w_sp, b_sp_full):
    n_tok = x2.shape[0]
    n_tiles = n_tok // IN_TM
    const = dict(pipeline_mode=pl.Buffered(1))
    done = lambda i: jnp.maximum(i - 1, 0)
    row_tile = lambda width: pl.BlockSpec((IN_TM, width), lambda i: (done(i), 0))
    whole = lambda shape: pl.BlockSpec(shape, lambda i: (0,) * len(shape), **const)
    return pl.pallas_call(
        _in_proj_kernel,
        grid=(n_tiles + 1,),
        in_specs=[
            pl.BlockSpec((IN_TM, D_MODEL), lambda i: (jnp.minimum(i, n_tiles - 1), 0)),
            whole((1, D_MODEL)),
            whole((D_MODEL, D_IN_PROJ)),
            whole((1, D_QKV)),
            pl.BlockSpec((CHUNK, LANES), lambda i: (done(i), 0)),
            whole((1, LANES)),
            whole((1, D_GMLP)), whole((1, D_GMLP)),
            whole((GROUPS, CHUNK, CHUNK)),
            whole((CHUNK, D_GMLP)),
        ],
        out_specs=[
            row_tile(D_GMLP),
            pl.BlockSpec((N_PAIRS, IN_NB, LANES, CHUNK), lambda i: (0, done(i), 0, 0)),
            row_tile(D_ATTN), row_tile(2 * D_KV)],
        out_shape=[
            jax.ShapeDtypeStruct((n_tok, D_GMLP), BF16),
            jax.ShapeDtypeStruct((N_PAIRS, n_tok // CHUNK, LANES, CHUNK), BF16),
            jax.ShapeDtypeStruct((n_tok, D_ATTN), BF16),
            jax.ShapeDtypeStruct((n_tok, 2 * D_KV), BF16),
        ],
        scratch_shapes=[pltpu.VMEM((IN_TM, D_MODEL), BF16), pltpu.VMEM((IN_TM, D_MODEL), BF16)],
        compiler_params=pltpu.CompilerParams(
            dimension_semantics=("arbitrary",), vmem_limit_bytes=VMEM_LIMIT),
        name="in_proj",
    )(x2, g_pre, w_in_b, b_qkv, pos_rep, invf, ln_g, ln_b, w_sp, b_sp_full)


ATT_TB = 512
ATT_NB = ATT_TB // CHUNK


def _attention_kernel(sink_ref, qt_ref, szb_ref, kv_ref, o_ref, k_s, vt_s):
    j = pl.program_id(1)
    row = lax.broadcasted_iota(jnp.int32, (CHUNK, LANES), 0)
    lane = lax.broadcasted_iota(jnp.int32, (CHUNK, LANES), 1)
    low_half = lane < HEAD_DIM
    prev_in_window = row > lane
    prev_f = prev_in_window.astype(BF16)
    cur_f = (row <= lane).astype(BF16)

    @pl.when(j == 0)
    def _():
        k_s[:, 0:CHUNK, :] = jnp.zeros((4, CHUNK, LANES), BF16)
        vt_s[0] = jnp.zeros((LANES, CHUNK), BF16)

    no_prev = jnp.where(j > 0, 0.0, -jnp.inf).astype(F32)

    for n in range(ATT_NB):
        rows = slice(n * CHUNK, (n + 1) * CHUNK)
        srows = slice((n + 1) * CHUNK, (n + 2) * CHUNK)
        kr = kv_ref[rows, 0:LANES].astype(F32)
        ke0 = jnp.where(low_half, kr, 0.0)
        ko1 = jnp.where(low_half, 0.0, kr)
        k_s[0, srows, :] = ke0.astype(BF16)
        k_s[1, srows, :] = pltpu.roll(ke0, HEAD_DIM, 1).astype(BF16)
        k_s[2, srows, :] = pltpu.roll(ko1, HEAD_DIM, 1).astype(BF16)
        k_s[3, srows, :] = ko1.astype(BF16)
        v = kv_ref[rows, LANES:2 * LANES].astype(F32)
        vt_s[n + 1] = v.T.astype(BF16)

    groups = [(n, h, pp) for n in range(ATT_NB) for h in range(N_KV_HEADS)
              for pp in range(PAIRS_PER_KV // 2)]

    def scores(n, h, pp):
        krows = slice(n * CHUNK, (n + 2) * CHUNK)
        kk = jnp.concatenate([k_s[2 * h, krows, :], k_s[2 * h + 1, krows, :]], axis=0)
        pair0 = h * PAIRS_PER_KV + 2 * pp
        q2t = jnp.concatenate([qt_ref[pair0, n], qt_ref[pair0 + 1, n]], axis=1)
        return jnp.dot(kk, q2t, preferred_element_type=F32)

    def finish(st, n, h, pp):
        rows = slice(n * CHUNK, (n + 1) * CHUNK)
        pair0 = h * PAIRS_PER_KV + 2 * pp
        vt = jnp.concatenate([vt_s[n], vt_s[n + 1]], axis=1)[h * HEAD_DIM:(h + 1) * HEAD_DIM, :]
        for c in range(2):
            probs, recip = [], []
            for parity in range(2):
                k0 = parity * 2 * CHUNK
                s_prev = st[k0:k0 + CHUNK, c * CHUNK:(c + 1) * CHUNK]
                if n == 0:
                    s_prev = s_prev + no_prev
                s = jnp.where(prev_in_window, s_prev,
                              st[k0 + CHUNK:k0 + 2 * CHUNK, c * CHUNK:(c + 1) * CHUNK])
                sink = sink_ref[2 * (pair0 + c) + parity] * LOG2E
                m = jnp.maximum(jnp.max(s, axis=0, keepdims=True), sink)
                e = jnp.exp2(s - m)
                den = jnp.sum(e, axis=0, keepdims=True) + jnp.exp2(sink - m)
                recip.append(1.0 / den)
                e16 = e.astype(BF16)
                probs.append(jnp.concatenate([e16 * prev_f, e16 * cur_f], axis=0))
            ot = jnp.dot(vt, jnp.concatenate(probs, axis=1), preferred_element_type=F32)
            yt = jnp.concatenate([ot[:, 0:CHUNK] * recip[0], ot[:, CHUNK:2 * CHUNK] * recip[1]],
                                 axis=0)
            cols = slice((pair0 + c) * LANES, (pair0 + c + 1) * LANES)
            o_ref[rows, cols] = (yt.T * szb_ref[rows, cols].astype(F32)).astype(BF16)

    st = scores(*groups[0])
    for i in range(1, len(groups) + 1):
        st_next = scores(*groups[i]) if i < len(groups) else None
        finish(st, *groups[i - 1])
        st = st_next

    k_s[:, 0:CHUNK, :] = k_s[:, ATT_TB:ATT_TB + CHUNK, :]
    vt_s[0] = vt_s[ATT_NB]


def _attention(q_t, szb, kv, sinks, batch, seq):
    n_tok = batch * seq
    tiles = seq // ATT_TB
    row_tile = lambda width: pl.BlockSpec((ATT_TB, width), lambda b, j, s: (b * tiles + j, 0))
    grid_spec = pltpu.PrefetchScalarGridSpec(
        num_scalar_prefetch=1,
        grid=(batch, tiles),
        in_specs=[
            pl.BlockSpec((N_PAIRS, ATT_NB, LANES, CHUNK), lambda b, j, s: (0, b * tiles + j, 0, 0)),
            row_tile(D_ATTN), row_tile(2 * D_KV)],
        out_specs=row_tile(D_ATTN),
        scratch_shapes=[
            pltpu.VMEM((4, ATT_TB + CHUNK, LANES), BF16),
            pltpu.VMEM((ATT_NB + 1, LANES, CHUNK), BF16),
        ],
    )
    return pl.pallas_call(
        _attention_kernel,
        grid_spec=grid_spec,
        out_shape=jax.ShapeDtypeStruct((n_tok, D_ATTN), BF16),
        compiler_params=pltpu.CompilerParams(
            dimension_semantics=("arbitrary", "arbitrary"), vmem_limit_bytes=VMEM_LIMIT),
        name="attention",
    )(sinks, q_t, szb, kv)


OUT_TM = 512
OUT_ROWS = 256


def _out_proj_kernel(ya_ref, yb_ref, x_ref, w_ref, g_ref, o_ref):
    for r in range(OUT_TM // OUT_ROWS):
        rows = slice(r * OUT_ROWS, (r + 1) * OUT_ROWS)
        y = jnp.dot(ya_ref[rows, :], w_ref[0:D_GMLP, :], preferred_element_type=F32)
        y = y + jnp.dot(yb_ref[rows, :], w_ref[D_GMLP:D_GMLP + D_ATTN, :],
                        preferred_element_type=F32)
        yn = y * lax.rsqrt(jnp.mean(y * y, axis=-1, keepdims=True) + EPS)
        o_ref[rows, :] = x_ref[rows, :] + yn * g_ref[...]


def _out_proj(y_a, y_b, x2, w_out_b, g_post):
    n_tok = x2.shape[0]
    const = dict(pipeline_mode=pl.Buffered(1))
    return pl.pallas_call(
        _out_proj_kernel,
        grid=(n_tok // OUT_TM,),
        in_specs=[
            pl.BlockSpec((OUT_TM, D_GMLP), lambda i: (i, 0)),
            pl.BlockSpec((OUT_TM, D_ATTN), lambda i: (i, 0)),
            pl.BlockSpec((OUT_TM, D_MODEL), lambda i: (i, 0)),
            pl.BlockSpec((D_GMLP + D_ATTN, D_MODEL), lambda i: (0, 0), **const),
            pl.BlockSpec((1, D_MODEL), lambda i: (0, 0), **const),
        ],
        out_specs=pl.BlockSpec((OUT_TM, D_MODEL), lambda i: (i, 0)),
        out_shape=jax.ShapeDtypeStruct((n_tok, D_MODEL), F32),
        compiler_params=pltpu.CompilerParams(
            dimension_semantics=("arbitrary",), vmem_limit_bytes=VMEM_LIMIT),
        name="out_proj",
    )(y_a, y_b, x2, w_out_b, g_post)


def kernel(x, positions, g_pre, w_in, b_qkv, ln_v_g, ln_v_b, w_spatial, b_spatial, attn_sinks,
           w_out, g_post):
    batch, seq, d_model = x.shape
    depth = w_in.shape[0]
    assert d_model == D_MODEL and seq % ATT_TB == 0 and seq % IN_TM == 0
    assert (batch * seq) % OUT_TM == 0 and w_in.shape[-1] == D_IN_PROJ

    pos_rep = _rope_positions(positions.reshape(-1))
    inv_freq = ROPE_THETA ** (-jnp.arange(HALF, dtype=F32) * (2.0 / HEAD_DIM))
    invf = jnp.tile(inv_freq, LANE_GROUPS).reshape(1, LANES)
    x2 = x.reshape(batch * seq, D_MODEL)
    for l in range(depth):
        b_sp_full = jnp.repeat(jnp.transpose(b_spatial[l]), CHUNK, axis=1)
        y_a, q_t, szb, kv = _in_proj(
            x2, g_pre[l].reshape(1, D_MODEL), w_in[l].astype(BF16), b_qkv[l].reshape(1, D_QKV),
            pos_rep, invf, ln_v_g[l].reshape(1, D_GMLP), ln_v_b[l].reshape(1, D_GMLP),
            w_spatial[l], b_sp_full)
        y_b = _attention(q_t, szb, kv, attn_sinks[l], batch, seq)
        x2 = _out_proj(y_a, y_b, x2, w_out[l].astype(BF16), g_post[l].reshape(1, D_MODEL))
    return x2.reshape(batch, seq, D_MODEL)
```

```python
import functools

import jax
import jax.numpy as jnp
from jax import lax
from jax.experimental import pallas as pl
from jax.experimental.pallas import tpu as pltpu

F32 = jnp.float32
BF16 = jnp.bfloat16

D_MODEL = 2048
D_GMLP = 1024
D_ATTN = 1024
CHUNK = 128
GROUPS = 8
HEAD_DIM = 64
HALF = HEAD_DIM // 2
N_Q_HEADS = 16
N_KV_HEADS = 2
N_PAIRS = N_Q_HEADS // 2
PAIRS_PER_KV = N_PAIRS // N_KV_HEADS
D_KV = N_KV_HEADS * HEAD_DIM
D_QKV = D_ATTN + 2 * D_KV
D_IN_PROJ = 3 * D_GMLP + D_QKV + D_ATTN
ROPE_THETA = 10000.0
EPS = 1e-6
LOG2E = 1.4426950408889634
LANES = 128
LANE_GROUPS = LANES // HALF

COL_U, COL_VG, COL_ZA = 0, D_GMLP, 2 * D_GMLP
COL_Q = 3 * D_GMLP
COL_KV = COL_Q + D_ATTN
COL_ZB = COL_KV + 2 * D_KV

VMEM_LIMIT = 56 * 1024 * 1024


def _silu(z):
    return z * (1.0 / (1.0 + jnp.exp(-z)))


IN_TM = 4 * CHUNK
IN_NB = IN_TM // CHUNK
IN_ROWS = 128
IN_TN = 512
assert IN_NB == LANE_GROUPS


def _rope_positions(positions):
    n_tok = positions.size
    pos = positions.reshape(n_tok // IN_TM, LANE_GROUPS, CHUNK)
    return jnp.repeat(jnp.swapaxes(pos, 1, 2).reshape(n_tok // LANE_GROUPS, LANE_GROUPS),
                      HALF, axis=1)


def _expand_table(t):
    lane_group = lax.broadcasted_iota(jnp.int32, (CHUNK, LANES), 1) // HALF
    blocks = []
    for n in range(IN_NB):
        b = jnp.where(lane_group == n, t, 0.0)
        b = b + pltpu.roll(b, HALF, 1)
        blocks.append(b + pltpu.roll(b, 2 * HALF, 1))
    return jnp.concatenate(blocks, axis=0)


def _rope(x, cos, sin_hi, sin_lo):
    return x * cos + pltpu.roll(x, LANES - HALF, 1) * sin_hi + pltpu.roll(x, HALF, 1) * sin_lo


def _in_proj_kernel(x_ref, g_ref, w_ref, b_ref, pos_ref, invf_ref, lng_ref, lnb_ref, wsp_ref, bsp_ref,
                    ya_ref, qt_ref, szb_ref, kv_ref, h0_ref, h1_ref):
    i = pl.program_id(0)

    def normalize(h_ref):
        for r in range(0, IN_TM, IN_ROWS):
            x = x_ref[r:r + IN_ROWS, :]
            y = x * lax.rsqrt(jnp.mean(x * x, axis=-1, keepdims=True) + EPS)
            h_ref[r:r + IN_ROWS, :] = (y * g_ref[...]).astype(BF16)

    def project(h_ref):
        def proj(col, width):
            return jnp.dot(h_ref[...], w_ref[:, col:col + width], preferred_element_type=F32)

        vg = proj(COL_VG, D_GMLP)
        mu = jnp.mean(vg, axis=-1, keepdims=True)
        vc = vg - mu
        var = jnp.mean(vc * vc, axis=-1, keepdims=True)
        vn = (vc * lax.rsqrt(var + EPS) * lng_ref[...] + lnb_ref[...]).astype(BF16)
        wrow = lax.broadcasted_iota(jnp.int32, (CHUNK, CHUNK), 0)
        wcol = lax.broadcasted_iota(jnp.int32, (CHUNK, CHUNK), 1)
        w_causal = [jnp.where(wrow >= wcol, wsp_ref[g], 0.0).astype(BF16) for g in range(GROUPS)]
        for c in range(0, D_GMLP, IN_TN):
            u = proj(COL_U + c, IN_TN)
            sza = _silu(proj(COL_ZA + c, IN_TN))
            for n in range(IN_NB):
                rows = slice(n * CHUNK, (n + 1) * CHUNK)
                for g in range(c // CHUNK, (c + IN_TN) // CHUNK):
                    cols = slice(g * CHUNK, (g + 1) * CHUNK)
                    lcols = slice(g * CHUNK - c, (g + 1) * CHUNK - c)
                    mixed = jnp.dot(w_causal[g], vn[rows, cols], preferred_element_type=F32)
                    mixed = mixed + bsp_ref[:, cols]
                    ya_ref[rows, cols] = (u[rows, lcols] * mixed * sza[rows, lcols]).astype(BF16)

        ang = pos_ref[...].astype(F32) * invf_ref[...]
        cos = _expand_table(jnp.cos(ang))
        sin = _expand_table(jnp.sin(ang))
        first_half = (lax.broadcasted_iota(jnp.int32, (IN_TM, LANES), 1) // HALF) % 2 == 0
        sin_hi = jnp.where(first_half, -sin, 0.0)
        sin_lo = jnp.where(first_half, 0.0, sin)
        q_scale = LOG2E * HEAD_DIM ** -0.5
        cq, sq_hi, sq_lo = cos * q_scale, sin_hi * q_scale, sin_lo * q_scale
        for c in range(0, D_ATTN, IN_TN):
            q = proj(COL_Q + c, IN_TN) + b_ref[:, c:c + IN_TN]
            for s in range(0, IN_TN, LANES):
                roped = _rope(q[:, s:s + LANES], cq, sq_hi, sq_lo)
                for n in range(IN_NB):
                    qt_ref[(c + s) // LANES, n] = roped[n * CHUNK:(n + 1) * CHUNK, :].T.astype(BF16)
        kv = proj(COL_KV, 2 * D_KV) + b_ref[:, D_ATTN:D_QKV]
        kv_ref[:, 0:D_KV] = _rope(kv[:, 0:D_KV], cos, sin_hi, sin_lo).astype(BF16)
        kv_ref[:, D_KV:2 * D_KV] = kv[:, D_KV:2 * D_KV].astype(BF16)
        for c in range(0, D_ATTN, IN_TN):
            szb_ref[:, c:c + IN_TN] = _silu(proj(COL_ZB + c, IN_TN)).astype(BF16)

    @pl.when(i == 0)
    def _():
        normalize(h0_ref)

    @pl.when(i % 2 == 1)
    def _():
        normalize(h1_ref)
        project(h0_ref)

    @pl.when((i > 0) & (i % 2 == 0))
    def _():
        normalize(h0_ref)
        project(h1_ref)


def _in_proj(x2, g_pre, w_in_b, b_qkv, pos_rep, invf, ln_g, ln_b, w_sp, b_sp_full):
    n_tok = x2.shape[0]
    n_tiles = n_tok // IN_TM
    const = dict(pipeline_mode=pl.Buffered(1))
    done = lambda i: jnp.maximum(i - 1, 0)
    row_tile = lambda width: pl.BlockSpec((IN_TM, width), lambda i: (done(i), 0))
    whole = lambda shape: pl.BlockSpec(shape, lambda i: (0,) * len(shape), **const)
    return pl.pallas_call(
        _in_proj_kernel,
        grid=(n_tiles + 1,),
        in_specs=[
            pl.BlockSpec((IN_TM, D_MODEL), lambda i: (jnp.minimum(i, n_tiles - 1), 0)),
            whole((1, D_MODEL)),
            whole((D_MODEL, D_IN_PROJ)),
            whole((1, D_QKV)),
            pl.BlockSpec((CHUNK, LANES), lambda i: (done(i), 0)),
            whole((1, LANES)),
            whole((1, D_GMLP)), whole((1, D_GMLP)),
            whole((GROUPS, CHUNK, CHUNK)),
            whole((CHUNK, D_GMLP)),
        ],
        out_specs=[
            row_tile(D_GMLP),
            pl.BlockSpec((N_PAIRS, IN_NB, LANES, CHUNK), lambda i: (0, done(i), 0, 0)),
            row_tile(D_ATTN), row_tile(2 * D_KV)],
        out_shape=[
            jax.ShapeDtypeStruct((n_tok, D_GMLP), BF16),
            jax.ShapeDtypeStruct((N_PAIRS, n_tok // CHUNK, LANES, CHUNK), BF16),
            jax.ShapeDtypeStruct((n_tok, D_ATTN), BF16),
            jax.ShapeDtypeStruct((n_tok, 2 * D_KV), BF16),
        ],
        scratch_shapes=[pltpu.VMEM((IN_TM, D_MODEL), BF16), pltpu.VMEM((IN_TM, D_MODEL), BF16)],
        compiler_params=pltpu.CompilerParams(
            dimension_semantics=("arbitrary",), vmem_limit_bytes=VMEM_LIMIT),
        name="in_proj",
    )(x2, g_pre, w_in_b, b_qkv, pos_rep, invf, ln_g, ln_b, w_sp, b_sp_full)


AO_TM = 512
AO_NB = AO_TM // CHUNK
AO_TN = 256
AO_PIECES = D_MODEL // AO_TN


def _attn_out_kernel(sink_ref, qt_ref, szb_ref, kv_ref, ya_ref, x_ref, w_ref, g_ref, o_ref,
                     k_s, vt_s, yb0_s, yb1_s, y_s, *, tiles_per_seq):
    i = pl.program_id(0)
    n_tiles = pl.num_programs(0) - 1

    def attention(yb_ref, pieces):
        row = lax.broadcasted_iota(jnp.int32, (CHUNK, LANES), 0)
        lane = lax.broadcasted_iota(jnp.int32, (CHUNK, LANES), 1)
        low_half = lane < HEAD_DIM
        prev_in_window = row > lane
        prev_f = prev_in_window.astype(BF16)
        cur_f = (row <= lane).astype(BF16)
        first_of_seq = i % tiles_per_seq == 0

        @pl.when(first_of_seq)
        def _():
            k_s[:, 0:CHUNK, :] = jnp.zeros((4, CHUNK, LANES), BF16)
            vt_s[0] = jnp.zeros((LANES, CHUNK), BF16)

        no_prev = jnp.where(first_of_seq, -jnp.inf, 0.0).astype(F32)

        for n in range(AO_NB):
            rows = slice(n * CHUNK, (n + 1) * CHUNK)
            srows = slice((n + 1) * CHUNK, (n + 2) * CHUNK)
            kr = kv_ref[rows, 0:LANES].astype(F32)
            ke0 = jnp.where(low_half, kr, 0.0)
            ko1 = jnp.where(low_half, 0.0, kr)
            k_s[0, srows, :] = ke0.astype(BF16)
            k_s[1, srows, :] = pltpu.roll(ke0, HEAD_DIM, 1).astype(BF16)
            k_s[2, srows, :] = pltpu.roll(ko1, HEAD_DIM, 1).astype(BF16)
            k_s[3, srows, :] = ko1.astype(BF16)
            v = kv_ref[rows, LANES:2 * LANES].astype(F32)
            vt_s[n + 1] = v.T.astype(BF16)

        groups = [(n, h, pp) for n in range(AO_NB) for h in range(N_KV_HEADS)
                  for pp in range(PAIRS_PER_KV // 2)]

        def scores(n, h, pp):
            krows = slice(n * CHUNK, (n + 2) * CHUNK)
            kk = jnp.concatenate([k_s[2 * h, krows, :], k_s[2 * h + 1, krows, :]], axis=0)
            pair0 = h * PAIRS_PER_KV + 2 * pp
            q2t = jnp.concatenate([qt_ref[pair0, n], qt_ref[pair0 + 1, n]], axis=1)
            return jnp.dot(kk, q2t, preferred_element_type=F32)

        def finish(st, n, h, pp):
            rows = slice(n * CHUNK, (n + 1) * CHUNK)
            pair0 = h * PAIRS_PER_KV + 2 * pp
            vt = jnp.concatenate([vt_s[n], vt_s[n + 1]], axis=1)[h * HEAD_DIM:(h + 1) * HEAD_DIM, :]
            for c in range(2):
                probs, recip = [], []
                for parity in range(2):
                    k0 = parity * 2 * CHUNK
                    s_prev = st[k0:k0 + CHUNK, c * CHUNK:(c + 1) * CHUNK]
                    if n == 0:
                        s_prev = s_prev + no_prev
                    s = jnp.where(prev_in_window, s_prev,
                                  st[k0 + CHUNK:k0 + 2 * CHUNK, c * CHUNK:(c + 1) * CHUNK])
                    sink = sink_ref[2 * (pair0 + c) + parity] * LOG2E
                    m = jnp.maximum(jnp.max(s, axis=0, keepdims=True), sink)
                    e = jnp.exp2(s - m)
                    den = jnp.sum(e, axis=0, keepdims=True) + jnp.exp2(sink - m)
                    recip.append(1.0 / den)
                    e16 = e.astype(BF16)
                    probs.append(jnp.concatenate([e16 * prev_f, e16 * cur_f], axis=0))
                ot = jnp.dot(vt, jnp.concatenate(probs, axis=1), preferred_element_type=F32)
                yt = jnp.concatenate([ot[:, 0:CHUNK] * recip[0], ot[:, CHUNK:2 * CHUNK] * recip[1]],
                                     axis=0)
                cols = slice((pair0 + c) * LANES, (pair0 + c + 1) * LANES)
                yb_ref[rows, cols] = (yt.T * szb_ref[rows, cols].astype(F32)).astype(BF16)

        every = len(groups) // len(pieces) if pieces else 0
        st = scores(*groups[0])
        for t in range(1, len(groups) + 1):
            st_next = scores(*groups[t]) if t < len(groups) else None
            finish(st, *groups[t - 1])
            if pieces and t % every == 0:
                pieces[t // every - 1]()
            st = st_next

        k_s[:, 0:CHUNK, :] = k_s[:, AO_TM:AO_TM + CHUNK, :]
        vt_s[0] = vt_s[AO_NB]

    def out_pieces(yb_ref):
        def piece(c):
            cols = slice(c * AO_TN, (c + 1) * AO_TN)
            y = jnp.dot(ya_ref[...], w_ref[0:D_GMLP, cols], preferred_element_type=F32)
            y_s[:, cols] = y + jnp.dot(yb_ref[...], w_ref[D_GMLP:D_GMLP + D_ATTN, cols],
                                       preferred_element_type=F32)
        return [lambda c=c: piece(c) for c in range(AO_PIECES)]

    def out_epilogue():
        for r in range(0, AO_TM, CHUNK):
            y = y_s[r:r + CHUNK, :]
            yn = y * lax.rsqrt(jnp.mean(y * y, axis=-1, keepdims=True) + EPS)
            o_ref[r:r + CHUNK, :] = x_ref[r:r + CHUNK, :] + yn * g_ref[...]

    @pl.when(i == 0)
    def _():
        attention(yb0_s, [])

    @pl.when((i % 2 == 1) & (i < n_tiles))
    def _():
        attention(yb1_s, out_pieces(yb0_s))
        out_epilogue()

    @pl.when((i > 0) & (i % 2 == 0) & (i < n_tiles))
    def _():
        attention(yb0_s, out_pieces(yb1_s))
        out_epilogue()

    @pl.when((i == n_tiles) & (n_tiles % 2 == 1))
    def _():
        for piece in out_pieces(yb0_s):
            piece()
        out_epilogue()

    @pl.when((i == n_tiles) & (n_tiles % 2 == 0))
    def _():
        for piece in out_pieces(yb1_s):
            piece()
        out_epilogue()


def _attn_out(q_t, szb, kv, y_a, x2, w_out_b, g_post, sinks, seq):
    n_tok = x2.shape[0]
    n_tiles = n_tok // AO_TM
    const = dict(pipeline_mode=pl.Buffered(1))
    cur = lambda i, s: jnp.minimum(i, n_tiles - 1)
    done = lambda i, s: jnp.maximum(i - 1, 0)
    grid_spec = pltpu.PrefetchScalarGridSpec(
        num_scalar_prefetch=1,
        grid=(n_tiles + 1,),
        in_specs=[
            pl.BlockSpec((N_PAIRS, AO_NB, LANES, CHUNK), lambda i, s: (0, cur(i, s), 0, 0)),
            pl.BlockSpec((AO_TM, D_ATTN), lambda i, s: (cur(i, s), 0)),
            pl.BlockSpec((AO_TM, 2 * D_KV), lambda i, s: (cur(i, s), 0)),
            pl.BlockSpec((AO_TM, D_GMLP), lambda i, s: (done(i, s), 0)),
            pl.BlockSpec((AO_TM, D_MODEL), lambda i, s: (done(i, s), 0)),
            pl.BlockSpec((D_GMLP + D_ATTN, D_MODEL), lambda i, s: (0, 0), **const),
            pl.BlockSpec((1, D_MODEL), lambda i, s: (0, 0), **const),
        ],
        out_specs=pl.BlockSpec((AO_TM, D_MODEL), lambda i, s: (done(i, s), 0)),
        scratch_shapes=[
            pltpu.VMEM((4, AO_TM + CHUNK, LANES), BF16),
            pltpu.VMEM((AO_NB + 1, LANES, CHUNK), BF16),
            pltpu.VMEM((AO_TM, D_ATTN), BF16),
            pltpu.VMEM((AO_TM, D_ATTN), BF16),
            pltpu.VMEM((AO_TM, D_MODEL), F32),
        ],
    )
    return pl.pallas_call(
        functools.partial(_attn_out_kernel, tiles_per_seq=seq // AO_TM),
        grid_spec=grid_spec,
        out_shape=jax.ShapeDtypeStruct((n_tok, D_MODEL), F32),
        compiler_params=pltpu.CompilerParams(
            dimension_semantics=("arbitrary",), vmem_limit_bytes=VMEM_LIMIT),
        name="attn_out",
    )(sinks, q_t, szb, kv, y_a, x2, w_out_b, g_post)


def kernel(x, positions, g_pre, w_in, b_qkv, ln_v_g, ln_v_b, w_spatial, b_spatial, attn_sinks,
           w_out, g_post):
    batch, seq, d_model = x.shape
    depth = w_in.shape[0]
    assert d_model == D_MODEL and seq % AO_TM == 0 and seq % IN_TM == 0
    assert w_in.shape[-1] == D_IN_PROJ

    pos_rep = _rope_positions(positions.reshape(-1))
    inv_freq = ROPE_THETA ** (-jnp.arange(HALF, dtype=F32) * (2.0 / HEAD_DIM))
    invf = jnp.tile(inv_freq, LANE_GROUPS).reshape(1, LANES)
    x2 = x.reshape(batch * seq, D_MODEL)
    for l in range(depth):
        b_sp_full = jnp.repeat(jnp.transpose(b_spatial[l]), CHUNK, axis=1)
        y_a, q_t, szb, kv = _in_proj(
            x2, g_pre[l].reshape(1, D_MODEL), w_in[l].astype(BF16), b_qkv[l].reshape(1, D_QKV),
            pos_rep, invf, ln_v_g[l].reshape(1, D_GMLP), ln_v_b[l].reshape(1, D_GMLP),
            w_spatial[l], b_sp_full)
        x2 = _attn_out(q_t, szb, kv, y_a, x2, w_out[l].astype(BF16), g_post[l].reshape(1, D_MODEL),
                       attn_sinks[l], seq)
    return x2.reshape(batch, seq, D_MODEL)
```

```python
import functools

import jax
import jax.numpy as jnp
from jax import lax
from jax.experimental import pallas as pl
from jax.experimental.pallas import tpu as pltpu

F32 = jnp.float32
BF16 = jnp.bfloat16

D_MODEL = 2048
D_GMLP = 1024
D_ATTN = 1024
CHUNK = 128
GROUPS = 8
HEAD_DIM = 64
HALF = HEAD_DIM // 2
N_Q_HEADS = 16
N_KV_HEADS = 2
N_PAIRS = N_Q_HEADS // 2
PAIRS_PER_KV = N_PAIRS // N_KV_HEADS
D_KV = N_KV_HEADS * HEAD_DIM
D_QKV = D_ATTN + 2 * D_KV
D_IN_PROJ = 3 * D_GMLP + D_QKV + D_ATTN
ROPE_THETA = 10000.0
EPS = 1e-6
LOG2E = 1.4426950408889634
LANES = 128
LANE_GROUPS = LANES // HALF
K_VARIANTS = 2 * N_KV_HEADS

COL_U, COL_VG, COL_ZA = 0, D_GMLP, 2 * D_GMLP
COL_Q = 3 * D_GMLP
COL_KV = COL_Q + D_ATTN
COL_ZB = COL_KV + 2 * D_KV

VMEM_LIMIT = 56 * 1024 * 1024


def _silu(z):
    return z * (1.0 / (1.0 + jnp.exp(-z)))


IN_TM = 4 * CHUNK
IN_NB = IN_TM // CHUNK
IN_ROWS = 128
IN_TN = 512
assert IN_NB == LANE_GROUPS


def _rope_positions(positions):
    n_tok = positions.size
    pos = positions.reshape(n_tok // IN_TM, LANE_GROUPS, CHUNK)
    return jnp.repeat(jnp.swapaxes(pos, 1, 2).reshape(n_tok // LANE_GROUPS, LANE_GROUPS),
                      HALF, axis=1)


def _expand_table(t):
    lane_group = lax.broadcasted_iota(jnp.int32, (CHUNK, LANES), 1) // HALF
    blocks = []
    for n in range(IN_NB):
        b = jnp.where(lane_group == n, t, 0.0)
        b = b + pltpu.roll(b, HALF, 1)
        blocks.append(b + pltpu.roll(b, 2 * HALF, 1))
    return jnp.concatenate(blocks, axis=0)


def _rope(x, cos, sin_hi, sin_lo):
    return x * cos + pltpu.roll(x, LANES - HALF, 1) * sin_hi + pltpu.roll(x, HALF, 1) * sin_lo


def _in_proj_kernel(x_ref, g_ref, w_ref, b_ref, pos_ref, invf_ref, lng_ref, lnb_ref, wsp_ref, bsp_ref,
                    ya_ref, qt_ref, szb_ref, kvar_ref, vt_ref, h0_ref, h1_ref):
    i = pl.program_id(0)

    def normalize(h_ref):
        for r in range(0, IN_TM, IN_ROWS):
            x = x_ref[r:r + IN_ROWS, :]
            y = x * lax.rsqrt(jnp.mean(x * x, axis=-1, keepdims=True) + EPS)
            h_ref[r:r + IN_ROWS, :] = (y * g_ref[...]).astype(BF16)

    def project(h_ref):
        def proj(col, width):
            return jnp.dot(h_ref[...], w_ref[:, col:col + width], preferred_element_type=F32)

        vg = proj(COL_VG, D_GMLP)
        mu = jnp.mean(vg, axis=-1, keepdims=True)
        vc = vg - mu
        var = jnp.mean(vc * vc, axis=-1, keepdims=True)
        vn = (vc * lax.rsqrt(var + EPS) * lng_ref[...] + lnb_ref[...]).astype(BF16)

        for c in range(0, D_ATTN, IN_TN):
            szb_ref[:, c:c + IN_TN] = _silu(proj(COL_ZB + c, IN_TN)).astype(BF16)
        ang = pos_ref[...].astype(F32) * invf_ref[...]
        cos = _expand_table(jnp.cos(ang))
        sin = _expand_table(jnp.sin(ang))
        first_half = (lax.broadcasted_iota(jnp.int32, (IN_TM, LANES), 1) // HALF) % 2 == 0
        sin_hi = jnp.where(first_half, -sin, 0.0)
        sin_lo = jnp.where(first_half, 0.0, sin)
        q_scale = LOG2E * HEAD_DIM ** -0.5
        cq, sq_hi, sq_lo = cos * q_scale, sin_hi * q_scale, sin_lo * q_scale
        for c in range(0, D_ATTN, IN_TN):
            q = proj(COL_Q + c, IN_TN) + b_ref[:, c:c + IN_TN]
            for s in range(0, IN_TN, LANES):
                roped = _rope(q[:, s:s + LANES], cq, sq_hi, sq_lo)
                for n in range(IN_NB):
                    qt_ref[(c + s) // LANES, n] = roped[n * CHUNK:(n + 1) * CHUNK, :].T.astype(BF16)

        wrow = lax.broadcasted_iota(jnp.int32, (CHUNK, CHUNK), 0)
        wcol = lax.broadcasted_iota(jnp.int32, (CHUNK, CHUNK), 1)
        w_causal = [jnp.where(wrow >= wcol, wsp_ref[g], 0.0).astype(BF16) for g in range(GROUPS)]
        for c in range(0, D_GMLP, IN_TN):
            u = proj(COL_U + c, IN_TN)
            sza = _silu(proj(COL_ZA + c, IN_TN))
            for n in range(IN_NB):
                rows = slice(n * CHUNK, (n + 1) * CHUNK)
                for g in range(c // CHUNK, (c + IN_TN) // CHUNK):
                    cols = slice(g * CHUNK, (g + 1) * CHUNK)
                    lcols = slice(g * CHUNK - c, (g + 1) * CHUNK - c)
                    mixed = jnp.dot(w_causal[g], vn[rows, cols], preferred_element_type=F32)
                    mixed = mixed + bsp_ref[:, cols]
                    ya_ref[rows, cols] = (u[rows, lcols] * mixed * sza[rows, lcols]).astype(BF16)

        kv = proj(COL_KV, 2 * D_KV) + b_ref[:, D_ATTN:D_QKV]
        kr = _rope(kv[:, 0:D_KV], cos, sin_hi, sin_lo)
        low_half = lax.broadcasted_iota(jnp.int32, (IN_TM, LANES), 1) < HEAD_DIM
        ke0 = jnp.where(low_half, kr, 0.0)
        ko1 = jnp.where(low_half, 0.0, kr)
        for var, k in enumerate([ke0, pltpu.roll(ke0, HEAD_DIM, 1), pltpu.roll(ko1, HEAD_DIM, 1), ko1]):
            kvar_ref[:, var * LANES:(var + 1) * LANES] = k.astype(BF16)
        for n in range(IN_NB):
            vt_ref[n] = kv[n * CHUNK:(n + 1) * CHUNK, D_KV:2 * D_KV].T.astype(BF16)

    @pl.when(i == 0)
    def _():
        normalize(h0_ref)

    @pl.when(i % 2 == 1)
    def _():
        normalize(h1_ref)
        project(h0_ref)

    @pl.when((i > 0) & (i % 2 == 0))
    def _():
        normalize(h0_ref)
        project(h1_ref)


def _in_proj(x2, g_pre, w_in_b, b_qkv, pos_rep, invf, ln_g, ln_b, w_sp, b_sp_full):
    n_tok = x2.shape[0]
    n_tiles = n_tok // IN_TM
    const = dict(pipeline_mode=pl.Buffered(1))
    done = lambda i: jnp.maximum(i - 1, 0)
    row_tile = lambda width: pl.BlockSpec((IN_TM, width), lambda i: (done(i), 0))
    whole = lambda shape: pl.BlockSpec(shape, lambda i: (0,) * len(shape), **const)
    return pl.pallas_call(
        _in_proj_kernel,
        grid=(n_tiles + 1,),
        in_specs=[
            pl.BlockSpec((IN_TM, D_MODEL), lambda i: (jnp.minimum(i, n_tiles - 1), 0)),
            whole((1, D_MODEL)),
            whole((D_MODEL, D_IN_PROJ)),
            whole((1, D_QKV)),
            pl.BlockSpec((CHUNK, LANES), lambda i: (done(i), 0)),
            whole((1, LANES)),
            whole((1, D_GMLP)), whole((1, D_GMLP)),
            whole((GROUPS, CHUNK, CHUNK)),
            whole((CHUNK, D_GMLP)),
        ],
        out_specs=[
            row_tile(D_GMLP),
            pl.BlockSpec((N_PAIRS, IN_NB, LANES, CHUNK), lambda i: (0, done(i), 0, 0)),
            row_tile(D_ATTN), row_tile(K_VARIANTS * LANES),
            pl.BlockSpec((IN_NB, LANES, CHUNK), lambda i: (done(i), 0, 0))],
        out_shape=[
            jax.ShapeDtypeStruct((n_tok, D_GMLP), BF16),
            jax.ShapeDtypeStruct((N_PAIRS, n_tok // CHUNK, LANES, CHUNK), BF16),
            jax.ShapeDtypeStruct((n_tok, D_ATTN), BF16),
            jax.ShapeDtypeStruct((n_tok, K_VARIANTS * LANES), BF16),
            jax.ShapeDtypeStruct((n_tok // CHUNK, LANES, CHUNK), BF16),
        ],
        scratch_shapes=[pltpu.VMEM((IN_TM, D_MODEL), BF16), pltpu.VMEM((IN_TM, D_MODEL), BF16)],
        compiler_params=pltpu.CompilerParams(
            dimension_semantics=("arbitrary",), vmem_limit_bytes=VMEM_LIMIT),
        name="in_proj",
    )(x2, g_pre, w_in_b, b_qkv, pos_rep, invf, ln_g, ln_b, w_sp, b_sp_full)


AO_TM = 512
AO_NB = AO_TM // CHUNK
AO_TN = 256
AO_PIECES = D_MODEL // AO_TN


def _attn_out_kernel(sink_ref, qt_ref, szb_ref, kvar_ref, vt_ref, ya_ref, x_ref, w_ref, g_ref, o_ref,
                     kc_s, vtc_s, yb0_s, yb1_s, y_s, ssq_s, *, tiles_per_seq, n_tiles):
    i = pl.program_id(0)

    def attention(yb_ref, between):
        row = lax.broadcasted_iota(jnp.int32, (CHUNK, LANES), 0)
        lane = lax.broadcasted_iota(jnp.int32, (CHUNK, LANES), 1)
        prev_in_window = row > lane
        prev_f = prev_in_window.astype(BF16)
        cur_f = (row <= lane).astype(BF16)
        first_of_seq = i % tiles_per_seq == 0

        @pl.when(first_of_seq)
        def _():
            kc_s[...] = jnp.zeros(kc_s.shape, BF16)
            vtc_s[...] = jnp.zeros(vtc_s.shape, BF16)

        no_prev = jnp.where(first_of_seq, -jnp.inf, 0.0).astype(F32)

        groups = [(n, h, pp) for n in range(AO_NB) for h in range(N_KV_HEADS)
                  for pp in range(PAIRS_PER_KV // 2)]

        def keys(n, var):
            cols = slice(var * LANES, (var + 1) * LANES)
            if n == 0:
                return jnp.concatenate([kc_s[:, cols], kvar_ref[0:CHUNK, cols]], axis=0)
            return kvar_ref[(n - 1) * CHUNK:(n + 1) * CHUNK, cols]

        def scores(n, h, pp):
            kk = jnp.concatenate([keys(n, 2 * h), keys(n, 2 * h + 1)], axis=0)
            pair0 = h * PAIRS_PER_KV + 2 * pp
            q2t = jnp.concatenate([qt_ref[pair0, n], qt_ref[pair0 + 1, n]], axis=1)
            return jnp.dot(kk, q2t, preferred_element_type=F32)

        def finish(st, n, h, pp):
            rows = slice(n * CHUNK, (n + 1) * CHUNK)
            pair0 = h * PAIRS_PER_KV + 2 * pp
            vt_prev = vtc_s[...] if n == 0 else vt_ref[n - 1]
            vt = jnp.concatenate([vt_prev, vt_ref[n]], axis=1)[h * HEAD_DIM:(h + 1) * HEAD_DIM, :]
            for c in range(2):
                probs, recip = [], []
                for parity in range(2):
                    k0 = parity * 2 * CHUNK
                    s_prev = st[k0:k0 + CHUNK, c * CHUNK:(c + 1) * CHUNK]
                    if n == 0:
                        s_prev = s_prev + no_prev
                    s = jnp.where(prev_in_window, s_prev,
                                  st[k0 + CHUNK:k0 + 2 * CHUNK, c * CHUNK:(c + 1) * CHUNK])
                    sink = sink_ref[2 * (pair0 + c) + parity] * LOG2E
                    m = jnp.maximum(jnp.max(s, axis=0, keepdims=True), sink)
                    e = jnp.exp2(s - m)
                    den = jnp.sum(e, axis=0, keepdims=True) + jnp.exp2(sink - m)
                    recip.append(1.0 / den)
                    e16 = e.astype(BF16)
                    probs.append(jnp.concatenate([e16 * prev_f, e16 * cur_f], axis=0))
                ot = jnp.dot(vt, jnp.concatenate(probs, axis=1), preferred_element_type=F32)
                yt = jnp.concatenate([ot[:, 0:CHUNK] * recip[0], ot[:, CHUNK:2 * CHUNK] * recip[1]],
                                     axis=0)
                cols = slice((pair0 + c) * LANES, (pair0 + c + 1) * LANES)
                yb_ref[rows, cols] = (yt.T * szb_ref[rows, cols].astype(F32)).astype(BF16)

        st = scores(*groups[0])
        for t in range(len(groups)):
            st_next = scores(*groups[t + 1]) if t + 1 < len(groups) else None
            finish(st, *groups[t])
            for emit in between.get(t, []):
                emit()
            st = st_next

        kc_s[...] = kvar_ref[AO_TM - CHUNK:AO_TM, :]
        vtc_s[...] = vt_ref[AO_NB - 1]

    def pieces(yb_ref):
        def piece(c):
            cols = slice(c * AO_TN, (c + 1) * AO_TN)
            y = jnp.dot(ya_ref[...], w_ref[0:D_GMLP, cols], preferred_element_type=F32)
            y = y + jnp.dot(yb_ref[...], w_ref[D_GMLP:D_GMLP + D_ATTN, cols],
                            preferred_element_type=F32)
            y_s[:, cols] = y
            sq = sum(y[:, s:s + LANES] * y[:, s:s + LANES] for s in range(0, AO_TN, LANES))
            ssq_s[...] = sq if c == 0 else ssq_s[...] + sq
        return [functools.partial(piece, c) for c in range(AO_PIECES)]

    def epilogue():
        def chunk(r):
            y = y_s[r:r + CHUNK, :]
            ms = jnp.sum(ssq_s[r:r + CHUNK, :], axis=-1, keepdims=True) * (1.0 / D_MODEL)
            yn = y * lax.rsqrt(ms + EPS)
            o_ref[r:r + CHUNK, :] = x_ref[r:r + CHUNK, :] + yn * g_ref[...]
        return [functools.partial(chunk, r) for r in range(0, AO_TM, CHUNK)]

    def interleaved(piece_list, chunk_list):
        n_groups = AO_NB * N_KV_HEADS * (PAIRS_PER_KV // 2)
        between = {}
        for c, emit in enumerate(piece_list):
            between.setdefault((c + 1) * n_groups // len(piece_list) - 1, []).append(emit)
        between.setdefault(n_groups - 1, []).extend(chunk_list)
        return between

    yb = (yb0_s, yb1_s)

    @pl.when(i == 0)
    def _():
        attention(yb[0], {})

    for parity in range(2):
        @pl.when((i >= 1) & (i < n_tiles) & (i % 2 == parity))
        def _():
            attention(yb[parity], interleaved(pieces(yb[1 - parity]), epilogue()))

    @pl.when(i == n_tiles)
    def _():
        for emit in pieces(yb[(n_tiles - 1) % 2]) + epilogue():
            emit()


def _attn_out(q_t, szb, kvar, v_t, y_a, x2, w_out_b, g_post, sinks, seq):
    n_tok = x2.shape[0]
    n_tiles = n_tok // AO_TM
    const = dict(pipeline_mode=pl.Buffered(1))
    tile = lambda i, lag: jnp.clip(i - lag, 0, n_tiles - 1)
    rows = lambda width, lag: pl.BlockSpec((AO_TM, width), lambda i, s: (tile(i, lag), 0))
    grid_spec = pltpu.PrefetchScalarGridSpec(
        num_scalar_prefetch=1,
        grid=(n_tiles + 1,),
        in_specs=[
            pl.BlockSpec((N_PAIRS, AO_NB, LANES, CHUNK), lambda i, s: (0, tile(i, 0), 0, 0)),
            rows(D_ATTN, 0),
            rows(K_VARIANTS * LANES, 0),
            pl.BlockSpec((AO_NB, LANES, CHUNK), lambda i, s: (tile(i, 0), 0, 0)),
            rows(D_GMLP, 1),
            rows(D_MODEL, 1),
            pl.BlockSpec((D_GMLP + D_ATTN, D_MODEL), lambda i, s: (0, 0), **const),
            pl.BlockSpec((1, D_MODEL), lambda i, s: (0, 0), **const),
        ],
        out_specs=rows(D_MODEL, 1),
        scratch_shapes=[
            pltpu.VMEM((CHUNK, K_VARIANTS * LANES), BF16),
            pltpu.VMEM((LANES, CHUNK), BF16),
            pltpu.VMEM((AO_TM, D_ATTN), BF16),
            pltpu.VMEM((AO_TM, D_ATTN), BF16),
            pltpu.VMEM((AO_TM, D_MODEL), F32),
            pltpu.VMEM((AO_TM, LANES), F32),
        ],
    )
    return pl.pallas_call(
        functools.partial(_attn_out_kernel, tiles_per_seq=seq // AO_TM, n_tiles=n_tiles),
        grid_spec=grid_spec,
        out_shape=jax.ShapeDtypeStruct((n_tok, D_MODEL), F32),
        compiler_params=pltpu.CompilerParams(
            dimension_semantics=("arbitrary",), vmem_limit_bytes=VMEM_LIMIT),
        name="attn_out",
    )(sinks, q_t, szb, kvar, v_t, y_a, x2, w_out_b, g_post)


def kernel(x, positions, g_pre, w_in, b_qkv, ln_v_g, ln_v_b, w_spatial, b_spatial, attn_sinks,
           w_out, g_post):
    batch, seq, d_model = x.shape
    depth = w_in.shape[0]
    assert d_model == D_MODEL and seq % AO_TM == 0 and seq % IN_TM == 0
    assert w_in.shape[-1] == D_IN_PROJ

    pos_rep = _rope_positions(positions.reshape(-1))
    inv_freq = ROPE_THETA ** (-jnp.arange(HALF, dtype=F32) * (2.0 / HEAD_DIM))
    invf = jnp.tile(inv_freq, LANE_GROUPS).reshape(1, LANES)
    x2 = x.reshape(batch * seq, D_MODEL)
    for l in range(depth):
        b_sp_full = jnp.repeat(jnp.transpose(b_spatial[l]), CHUNK, axis=1)
        y_a, q_t, szb, kvar, v_t = _in_proj(
            x2, g_pre[l].reshape(1, D_MODEL), w_in[l].astype(BF16), b_qkv[l].reshape(1, D_QKV),
            pos_rep, invf, ln_v_g[l].reshape(1, D_GMLP), ln_v_b[l].reshape(1, D_GMLP),
            w_spatial[l], b_sp_full)
        x2 = _attn_out(q_t, szb, kvar, v_t, y_a, x2, w_out[l].astype(BF16), g_post[l].reshape(1, D_MODEL),
                       attn_sinks[l], seq)
    return x2.reshape(batch, seq, D_MODEL)
```
